```python
import jax, jax.numpy as jnp
from jax import lax
import numpy as np

D_MODEL = 1024
BATCH = 8
SEQ = 4096
DEPTH = 2

HEAD_DIM = 64
MOBA_HEADS = 8
MOBA_BLOCK = 256
MOBA_TOPK = 3
MOBA_Q_CHUNK = 32
FOX_HEADS = 8
FOX_Q_BLOCK = 128
POOL_WINDOWS = (2, 4, 8, 16)
POOL_GROUP = 128
POOL_WIDTH = POOL_GROUP * len(POOL_WINDOWS)
MEM_HEADS = 4
MEM_HEAD_DIM = 128
MEM_LEN = 256
ROPE_THETA = 500000.0
ROT_DIM = HEAD_DIM // 4
N_BRANCH = 4
BRANCH_WIDTH = 512
D_FF = 2816
RMS_EPS = 1e-6
NEG_INF = -1e30

MOBA_WIDTH = MOBA_HEADS * HEAD_DIM
FOX_WIDTH = FOX_HEADS * HEAD_DIM
MEM_WIDTH = MEM_HEADS * MEM_HEAD_DIM
IN_SIZES = (3 * MOBA_WIDTH, 3 * FOX_WIDTH, FOX_HEADS, POOL_WIDTH, MEM_WIDTH, N_BRANCH * D_MODEL)
IN_SPLITS = tuple(sum(IN_SIZES[:i + 1]) for i in range(len(IN_SIZES) - 1))
D_IN = sum(IN_SIZES)

kernel_name = 'hybrid_moba_pool_fox_memory_macaron'


def rms_norm(x, gain):
    x32 = x.astype(jnp.float32)
    y = x32 * lax.rsqrt(jnp.mean(x32 * x32, axis=-1, keepdims=True) + RMS_EPS)
    return (y * gain.astype(jnp.float32)).astype(x.dtype)


def swiglu(h, w_in, w_out):
    g, u = jnp.split(h @ w_in, 2, axis=-1)
    return (jax.nn.silu(g) * u) @ w_out


def to_heads(t, n, d):
    b, s, _ = t.shape
    return t.reshape(b, s, n, d).transpose(0, 2, 1, 3)


def from_heads(t):
    b, n, s, d = t.shape
    return t.transpose(0, 2, 1, 3).reshape(b, s, n * d)


def split_qkv(t, n, d):
    b, s, _ = t.shape
    t = t.reshape(b, s, 3, n, d)
    return (t[:, :, 0].transpose(0, 2, 1, 3), t[:, :, 1].transpose(0, 2, 1, 3),
            t[:, :, 2].transpose(0, 2, 1, 3))


def partial_rotary(x, positions):
    half = ROT_DIM // 2
    inv_freq = jnp.power(ROPE_THETA, -jnp.arange(half, dtype=jnp.float32) * (2.0 / ROT_DIM))
    ang = positions.astype(jnp.float32)[:, None, :, None] * inv_freq
    cos = jnp.cos(ang).astype(x.dtype)
    sin = jnp.sin(ang).astype(x.dtype)
    x1 = x[..., :half]
    x2 = x[..., half:ROT_DIM]
    return jnp.concatenate([x1 * cos - x2 * sin, x2 * cos + x1 * sin, x[..., ROT_DIM:]], axis=-1)


def moba_attention(q, k, v):
    b, h, s, d = q.shape
    n_blk = -(-s // MOBA_BLOCK)
    s_pad = n_blk * MOBA_BLOCK
    pad = ((0, 0), (0, 0), (0, s_pad - s), (0, 0))
    q, k, v = (jnp.pad(t, pad) for t in (q, k, v))
    kb = k.reshape(b, h, n_blk, MOBA_BLOCK, d)
    vb = v.reshape(b, h, n_blk, MOBA_BLOCK, d)
    k_mean = jnp.mean(kb.astype(jnp.float32), axis=3)
    n_sel = min(MOBA_TOPK, n_blk)
    n_chunk = s_pad // MOBA_Q_CHUNK
    qc = q.reshape(b, h, n_chunk, MOBA_Q_CHUNK, d).transpose(2, 0, 1, 3, 4)
    b_idx = jnp.arange(b)[:, None, None, None]
    h_idx = jnp.arange(h)[None, :, None, None]
    scale = d ** -0.5

    def chunk(args):
        q_i, c_i = args
        start = c_i * MOBA_Q_CHUNK
        blk = start // MOBA_BLOCK
        q_pos = start + jnp.arange(MOBA_Q_CHUNK)
        gate = jnp.einsum('bhqd,bhnd->bhqn', q_i.astype(jnp.float32), k_mean)
        gate = jnp.where(jnp.arange(n_blk) < blk, gate, NEG_INF)
        _, sel = lax.top_k(gate, n_sel)
        valid = sel < blk
        k_sel = kb[b_idx, h_idx, sel]
        v_sel = vb[b_idx, h_idx, sel]
        s_sel = jnp.einsum('bhqd,bhqkld->bhqkl', q_i, k_sel).astype(jnp.float32) * scale
        s_sel = jnp.where(valid[..., None], s_sel, NEG_INF)
        s_sel = s_sel.reshape(b, h, MOBA_Q_CHUNK, n_sel * MOBA_BLOCK)
        k_own = lax.dynamic_index_in_dim(kb, blk, axis=2, keepdims=False)
        v_own = lax.dynamic_index_in_dim(vb, blk, axis=2, keepdims=False)
        s_own = jnp.einsum('bhqd,bhld->bhql', q_i, k_own).astype(jnp.float32) * scale
        k_pos = blk * MOBA_BLOCK + jnp.arange(MOBA_BLOCK)
        s_own = jnp.where(k_pos[None, :] <= q_pos[:, None], s_own, NEG_INF)
        p = jax.nn.softmax(jnp.concatenate([s_sel, s_own], axis=-1), axis=-1).astype(v.dtype)
        p_sel = p[..., :n_sel * MOBA_BLOCK].reshape(b, h, MOBA_Q_CHUNK, n_sel, MOBA_BLOCK)
        p_own = p[..., n_sel * MOBA_BLOCK:]
        return (jnp.einsum('bhqkl,bhqkld->bhqd', p_sel, v_sel)
                + jnp.einsum('bhql,bhld->bhqd', p_own, v_own))

    out = lax.map(chunk, (qc, jnp.arange(n_chunk)))
    out = out.transpose(1, 2, 0, 3, 4).reshape(b, h, s_pad, d)
    return out[:, :, :s]


def forgetting_attention(q, k, v, log_f):
    b, h, s, d = q.shape
    c = jnp.cumsum(log_f, axis=-1)
    n_blk = s // FOX_Q_BLOCK
    qb = q.reshape(b, h, n_blk, FOX_Q_BLOCK, d).transpose(2, 0, 1, 3, 4)
    cb = c.reshape(b, h, n_blk, FOX_Q_BLOCK).transpose(2, 0, 1, 3)
    k_pos = jnp.arange(s)
    scale = d ** -0.5

    def block(args):
        q_i, c_i, i = args
        q_pos = i * FOX_Q_BLOCK + jnp.arange(FOX_Q_BLOCK)
        logits = (jnp.einsum('bhqd,bhkd->bhqk', q_i, k).astype(jnp.float32) * scale
                  + (c_i[..., None] - c[:, :, None, :]))
        logits = jnp.where(k_pos[None, :] <= q_pos[:, None], logits, NEG_INF)
        p = jax.nn.softmax(logits, axis=-1).astype(v.dtype)
        return jnp.einsum('bhqk,bhkd->bhqd', p, v)

    out = lax.map(block, (qb, cb, jnp.arange(n_blk)))
    return out.transpose(1, 2, 0, 3, 4).reshape(b, h, s, d)


def multiscale_pool(u, pool_mix, pool_scale):
    b, s, _ = u.shape
    ug = u.reshape(b, s, len(POOL_WINDOWS), POOL_GROUP)
    csum = jnp.cumsum(ug.astype(jnp.float32), axis=1)
    t = jnp.arange(s)
    means = []
    for g, w in enumerate(POOL_WINDOWS):
        cg = csum[:, :, g]
        prev = jnp.pad(cg, ((0, 0), (w, 0), (0, 0)))[:, :s]
        cnt = jnp.minimum(t + 1, w).astype(jnp.float32)[None, :, None]
        means.append((cg - prev) / cnt)
    pooled = (jnp.stack(means, axis=2) - ug.astype(jnp.float32)).astype(u.dtype)
    mixed = jnp.einsum('bsgc,gce->bsge', pooled, pool_mix)
    return mixed.reshape(b, s, POOL_WIDTH) * pool_scale


def memory_cross_attention(q_flat, mem, norm_gain, w_kv, q_gain, k_gain):
    q = rms_norm(to_heads(q_flat, MEM_HEADS, MEM_HEAD_DIM), q_gain)
    k_m, v_m = jnp.split(rms_norm(mem, norm_gain) @ w_kv, 2, axis=-1)
    k_m = rms_norm(to_heads(k_m, MEM_HEADS, MEM_HEAD_DIM), k_gain)
    v_m = to_heads(v_m, MEM_HEADS, MEM_HEAD_DIM)
    logits = jnp.einsum('bhqd,bhmd->bhqm', q, k_m).astype(jnp.float32) * (MEM_HEAD_DIM ** -0.5)
    p = jax.nn.softmax(logits, axis=-1).astype(v_m.dtype)
    return from_heads(jnp.einsum('bhqm,bhmd->bhqd', p, v_m))


def hybrid_mixer(h, mem, positions, w_in, b_forget, moba_q_gain, moba_k_gain,
                 fox_q_gain, fox_k_gain, norm_mem, w_mem_kv, mem_q_gain, mem_k_gain,
                 pool_mix, pool_scale, w_branch, w_out):
    b, s, _ = h.shape
    proj = h @ w_in
    moba_qkv, fox_qkv, fox_f, pool_u, mem_q, gate_logits = jnp.split(proj, IN_SPLITS, axis=-1)
    q_a, k_a, v_a = split_qkv(moba_qkv, MOBA_HEADS, HEAD_DIM)
    q_a = partial_rotary(rms_norm(q_a, moba_q_gain), positions)
    k_a = partial_rotary(rms_norm(k_a, moba_k_gain), positions)
    o_a = from_heads(moba_attention(q_a, k_a, v_a))
    o_b = multiscale_pool(pool_u, pool_mix, pool_scale)
    q_c, k_c, v_c = split_qkv(fox_qkv, FOX_HEADS, HEAD_DIM)
    log_f = jax.nn.log_sigmoid((fox_f + b_forget).astype(jnp.float32)).transpose(0, 2, 1)
    o_c = from_heads(forgetting_attention(rms_norm(q_c, fox_q_gain), rms_norm(k_c, fox_k_gain),
                                          v_c, log_f))
    o_m = memory_cross_attention(mem_q, mem, norm_mem, w_mem_kv, mem_q_gain, mem_k_gain)
    branches = jnp.stack([o_a, o_b, o_c, o_m], axis=2)
    y = jnp.einsum('bsnc,ncd->bsnd', branches, w_branch)
    gates = jax.nn.sigmoid(gate_logits.reshape(b, s, N_BRANCH, D_MODEL))
    return jnp.sum(gates * y, axis=2) @ w_out


def setup_inputs(seed: int = 0) -> dict:
    key = jax.random.key(seed)
    ks = jax.random.split(key, 26)
    f32 = jnp.float32

    def dense(k, shape, fan_in):
        return jax.random.normal(k, shape, f32) * (fan_in ** -0.5)

    def gain(k, shape):
        return 1.0 + 0.05 * jax.random.normal(k, shape, f32)

    x = jax.random.normal(ks[0], (BATCH, SEQ, D_MODEL), f32)
    mem = jax.random.normal(ks[1], (BATCH, MEM_LEN, D_MODEL), f32)
    offsets = jax.random.randint(ks[2], (BATCH, 1), 0, 1024, dtype=jnp.int32)
    positions = offsets + jnp.arange(SEQ, dtype=jnp.int32)[None, :]
    L = DEPTH
    return {
        'x': x,
        'mem': mem,
        'positions': positions,
        'norm_ffn1': gain(ks[3], (L, D_MODEL)),
        'w_ffn1_in': dense(ks[4], (L, D_MODEL, 2 * D_FF), D_MODEL),
        'w_ffn1_out': dense(ks[5], (L, D_FF, D_MODEL), D_FF),
        'norm_mix': gain(ks[6], (L, D_MODEL)),
        'w_in': dense(ks[7], (L, D_MODEL, D_IN), D_MODEL),
        'b_forget': jax.random.uniform(ks[8], (L, FOX_HEADS), f32, 1.0, 5.0),
        'moba_q_gain': gain(ks[9], (L, HEAD_DIM)),
        'moba_k_gain': gain(ks[10], (L, HEAD_DIM)),
        'fox_q_gain': gain(ks[11], (L, HEAD_DIM)),
        'fox_k_gain': gain(ks[12], (L, HEAD_DIM)),
        'norm_mem': gain(ks[13], (L, D_MODEL)),
        'w_mem_kv': dense(ks[14], (L, D_MODEL, 2 * MEM_WIDTH), D_MODEL),
        'mem_q_gain': gain(ks[15], (L, MEM_HEAD_DIM)),
        'mem_k_gain': gain(ks[16], (L, MEM_HEAD_DIM)),
        'pool_mix': dense(ks[17], (L, len(POOL_WINDOWS), POOL_GROUP, POOL_GROUP), POOL_GROUP),
        'pool_scale': gain(ks[18], (L, POOL_WIDTH)),
        'w_branch': dense(ks[19], (L, N_BRANCH, BRANCH_WIDTH, D_MODEL), BRANCH_WIDTH),
        'w_out': dense(ks[20], (L, D_MODEL, D_MODEL), D_MODEL),
        'norm_ffn2': gain(ks[21], (L, D_MODEL)),
        'w_ffn2_in': dense(ks[22], (L, D_MODEL, 2 * D_FF), D_MODEL),
        'w_ffn2_out': dense(ks[23], (L, D_FF, D_MODEL), D_FF),
    }


def reference(x, mem, positions, norm_ffn1, w_ffn1_in, w_ffn1_out, norm_mix, w_in, b_forget,
              moba_q_gain, moba_k_gain, fox_q_gain, fox_k_gain, norm_mem, w_mem_kv,
              mem_q_gain, mem_k_gain, pool_mix, pool_scale, w_branch, w_out,
              norm_ffn2, w_ffn2_in, w_ffn2_out):
    for l in range(DEPTH):
        x = x + 0.5 * swiglu(rms_norm(x, norm_ffn1[l]), w_ffn1_in[l], w_ffn1_out[l])
        x = x + hybrid_mixer(rms_norm(x, norm_mix[l]), mem, positions, w_in[l], b_forget[l],
                             moba_q_gain[l], moba_k_gain[l], fox_q_gain[l], fox_k_gain[l],
                             norm_mem[l], w_mem_kv[l], mem_q_gain[l], mem_k_gain[l],
                             pool_mix[l], pool_scale[l], w_branch[l], w_out[l])
        x = x + 0.5 * swiglu(rms_norm(x, norm_ffn2[l]), w_ffn2_in[l], w_ffn2_out[l])
    return x
```

```python
import functools

import jax
import jax.numpy as jnp
from jax import lax
from jax.experimental import pallas as pl
from jax.experimental.pallas import tpu as pltpu

HEAD_DIM = 64
MOBA_HEADS = 8
MOBA_BLOCK = 256
MOBA_TOPK = 3
FOX_HEADS = 8
POOL_WINDOWS = (2, 4, 8, 16)
POOL_GROUP = 128
MEM_HEADS = 4
MEM_HEAD_DIM = 128
ROPE_THETA = 500000.0
ROT_DIM = HEAD_DIM // 4
N_BRANCH = 4
RMS_EPS = 1e-6
NEG_INF = -1e30

LANES = 128
HEADS_PER_GROUP = LANES // HEAD_DIM
POOL_HALO = 16
VMEM_LIMIT = 56 * 1024 * 1024

TM_FFN = 512
TM_PROJ = 512
TQ_FOX = 256
FFN_CHUNK = 256

F32 = jnp.float32
BF16 = jnp.bfloat16
_NT = (((1,), (1,)), ((), ()))


def _params():
    return pltpu.CompilerParams(vmem_limit_bytes=VMEM_LIMIT)


def _resident(shape):
    nd = len(shape)
    return pl.BlockSpec(shape, lambda *_: (0,) * nd, pipeline_mode=pl.Buffered(1))


def _rms(x, gain):
    ms = jnp.mean(x * x, axis=-1, keepdims=True)
    return x * lax.rsqrt(ms + RMS_EPS) * gain


def _head_pair_rms(x, gain):
    lane = lax.broadcasted_iota(jnp.int32, x.shape, 1)
    lo = lane < HEAD_DIM
    sq = x * x
    s_lo = jnp.sum(jnp.where(lo, sq, 0.0), axis=-1, keepdims=True)
    s_hi = jnp.sum(jnp.where(lo, 0.0, sq), axis=-1, keepdims=True)
    ms = jnp.where(lo, s_lo, s_hi) * (1.0 / HEAD_DIM)
    return x * lax.rsqrt(ms + RMS_EPS) * gain


def _rotate(x, cos, sin):
    lane = lax.broadcasted_iota(jnp.int32, x.shape, 1)
    half = ROT_DIM // 2
    fwd = pltpu.roll(x, LANES - half, axis=1)
    bwd = pltpu.roll(x, half, axis=1)
    partner = jnp.where((lane % HEAD_DIM) < half, fwd, bwd)
    return x * cos + partner * sin


def _rope_kernel(pos_ref, freq_ref, sign_ref, cos_ref, sin_ref):
    ang = pos_ref[0].astype(F32) * freq_ref[...]
    cos_ref[0] = jnp.cos(ang)
    sin_ref[0] = jnp.sin(ang) * sign_ref[...]


def _rope_tables(positions):
    b, s = positions.shape
    half = ROT_DIM // 2
    inv_freq = jnp.power(ROPE_THETA, -jnp.arange(half, dtype=F32) * (2.0 / ROT_DIM))
    lane = jnp.arange(LANES) % HEAD_DIM
    freq = jnp.where(lane < ROT_DIM, inv_freq[lane % half], 0.0).astype(F32)[None, :]
    sign = jnp.where(lane < half, -1.0, jnp.where(lane < ROT_DIM, 1.0, 0.0)).astype(F32)[None, :]
    ts = min(s, 1024)
    return pl.pallas_call(
        _rope_kernel,
        out_shape=(jax.ShapeDtypeStruct((b, s, LANES), F32),) * 2,
        grid=(b, s // ts),
        in_specs=[pl.BlockSpec((1, ts, 1), lambda i, j: (i, j, 0)),
                  pl.BlockSpec((1, LANES), lambda i, j: (0, 0)),
                  pl.BlockSpec((1, LANES), lambda i, j: (0, 0))],
        out_specs=(pl.BlockSpec((1, ts, LANES), lambda i, j: (i, j, 0)),) * 2,
        compiler_params=_params(),
        name="rope_tables",
    )(positions.reshape(b, s, 1), freq, sign)


def _ffn_kernel(x_ref, g_ref, wg_ref, wu_ref, wo_ref, o_ref):
    x = x_ref[...]
    h = _rms(x, g_ref[...]).astype(BF16)
    d_ff = wg_ref.shape[1]
    acc = jnp.zeros_like(x)
    for c in range(d_ff // FFN_CHUNK):
        sl = slice(c * FFN_CHUNK, (c + 1) * FFN_CHUNK)
        g = jnp.dot(h, wg_ref[:, sl], preferred_element_type=F32)
        u = jnp.dot(h, wu_ref[:, sl], preferred_element_type=F32)
        a = (g * jax.nn.sigmoid(g) * u).astype(BF16)
        acc = acc + jnp.dot(a, wo_ref[sl, :], preferred_element_type=F32)
    o_ref[...] = x + 0.5 * acc


def _ffn(x2, gain, w_in, w_out):
    t, d = x2.shape
    d_ff = w_out.shape[0]
    wg = w_in[:, :d_ff].astype(BF16)
    wu = w_in[:, d_ff:].astype(BF16)
    wo = w_out.astype(BF16)
    tm = min(TM_FFN, t)
    return pl.pallas_call(
        _ffn_kernel,
        out_shape=jax.ShapeDtypeStruct((t, d), F32),
        grid=(t // tm,),
        in_specs=[pl.BlockSpec((tm, d), lambda i: (i, 0)),
                  _resident((1, d)), _resident((d, d_ff)), _resident((d, d_ff)), _resident((d_ff, d))],
        out_specs=pl.BlockSpec((tm, d), lambda i: (i, 0)),
        compiler_params=_params(),
        name="ffn_half_step",
    )(x2, gain[None, :], wg, wu, wo)


def _memkv_kernel(mem_ref, gn_ref, w_ref, kg_ref, k_ref, v_ref):
    width = k_ref.shape[2]
    mn = _rms(mem_ref[0], gn_ref[...]).astype(BF16)
    kv = jnp.dot(mn, w_ref[...], preferred_element_type=F32)
    for h in range(MEM_HEADS):
        sl = slice(h * MEM_HEAD_DIM, (h + 1) * MEM_HEAD_DIM)
        k_ref[0, :, sl] = _rms(kv[:, sl], kg_ref[...]).astype(BF16)
    v_ref[0] = kv[:, width:].astype(BF16)


def _mem_kv(mem, norm_gain, w_kv, k_gain):
    b, m, d = mem.shape
    width = MEM_HEADS * MEM_HEAD_DIM
    return pl.pallas_call(
        _memkv_kernel,
        out_shape=(jax.ShapeDtypeStruct((b, m, width), BF16),) * 2,
        grid=(b,),
        in_specs=[pl.BlockSpec((1, m, d), lambda i: (i, 0, 0)),
                  _resident((1, d)), _resident((d, 2 * width)), _resident((1, MEM_HEAD_DIM))],
        out_specs=(pl.BlockSpec((1, m, width), lambda i: (i, 0, 0)),) * 2,
        compiler_params=_params(),
        name="memory_kv",
    )(mem, norm_gain[None, :], w_kv.astype(BF16), k_gain[None, :])


def _proj_kernel(x_ref, gn_ref, wqkv_ref, wf_ref, wp_ref, wmq_ref, bf_ref, gains_ref,
                 cos_ref, sin_ref, pmix_ref, pscale_ref, km_ref, vm_ref, mqg_ref,
                 qa_ref, ka_ref, va_ref, qc_ref, kc_ref, vc_ref, c_ref, ob_ref, om_ref, kmean_ref,
                 pool_carry, c_carry):
    j = pl.program_id(1)
    tm = x_ref.shape[1]
    width = qa_ref.shape[2]
    n_grp = width // LANES

    @pl.when(j == 0)
    def _():
        pool_carry[...] = jnp.zeros_like(pool_carry)
        c_carry[...] = jnp.zeros_like(c_carry)

    h = _rms(x_ref[0], gn_ref[...]).astype(BF16)
    scale = HEAD_DIM ** -0.5

    cos = cos_ref[0]
    sin = sin_ref[0]
    kmean_ref[...] = jnp.zeros_like(kmean_ref)
    for idx, (q_out, k_out, v_out, rot) in enumerate(((qa_ref, ka_ref, va_ref, True),
                                                      (qc_ref, kc_ref, vc_ref, False))):
        base = idx * 3 * width
        for g in range(n_grp):
            sl = slice(g * LANES, (g + 1) * LANES)
            q = jnp.dot(h, wqkv_ref[:, base + g * LANES: base + (g + 1) * LANES],
                        preferred_element_type=F32)
            k = jnp.dot(h, wqkv_ref[:, base + width + g * LANES: base + width + (g + 1) * LANES],
                        preferred_element_type=F32)
            v = jnp.dot(h, wqkv_ref[:, base + 2 * width + g * LANES: base + 2 * width + (g + 1) * LANES],
                        preferred_element_type=F32)
            q = _head_pair_rms(q, gains_ref[2 * idx: 2 * idx + 1, :])
            k = _head_pair_rms(k, gains_ref[2 * idx + 1: 2 * idx + 2, :])
            if rot:
                q = _rotate(q, cos, sin)
                k = _rotate(k, cos, sin)
                for sb in range(tm // MOBA_BLOCK):
                    kmean_ref[0, 0, sb:sb + 1, sl] = jnp.mean(
                        k[sb * MOBA_BLOCK:(sb + 1) * MOBA_BLOCK], axis=0, keepdims=True)
            q_out[0, :, sl] = (q * scale).astype(BF16)
            k_out[0, :, sl] = k.astype(BF16)
            v_out[0, :, sl] = v.astype(BF16)

    lf = jax.nn.log_sigmoid(jnp.dot(h, wf_ref[...], preferred_element_type=F32) + bf_ref[...])
    row = lax.broadcasted_iota(jnp.int32, lf.shape, 0)
    step = 1
    while step < tm:
        lf = lf + jnp.where(row >= step, pltpu.roll(lf, step, axis=0), 0.0)
        step *= 2
    c = lf + c_carry[...]
    c_ref[0] = c
    c_carry[...] = c[tm - 1:tm, :]

    u = jnp.dot(h, wp_ref[...], preferred_element_type=F32)
    ext = jnp.concatenate([pool_carry[...], u], axis=0)
    pos = j * tm + lax.broadcasted_iota(jnp.int32, (tm, 1), 0)
    for g, w in enumerate(POOL_WINDOWS):
        sl = slice(g * POOL_GROUP, (g + 1) * POOL_GROUP)
        win = ext[:, sl]
        step = 1
        while step < w:
            win = win + pltpu.roll(win, step, axis=0)
            step *= 2
        cnt = jnp.minimum(pos + 1, w).astype(F32)
        pooled = (win[POOL_HALO:, :] / cnt - u[:, sl]).astype(BF16)
        mixed = jnp.dot(pooled, pmix_ref[g], preferred_element_type=F32)
        ob_ref[0, :, sl] = (mixed * pscale_ref[:, sl]).astype(BF16)
    pool_carry[...] = u[tm - POOL_HALO:, :]

    for hd in range(MEM_HEADS):
        sl = slice(hd * MEM_HEAD_DIM, (hd + 1) * MEM_HEAD_DIM)
        qm = jnp.dot(h, wmq_ref[:, sl], preferred_element_type=F32)
        qm = _rms(qm, mqg_ref[...]).astype(BF16)
        s = lax.dot_general(qm, km_ref[0, :, sl], _NT, preferred_element_type=F32) * (MEM_HEAD_DIM ** -0.5)
        p = jnp.exp(s - jnp.max(s, axis=-1, keepdims=True))
        den = jnp.sum(p, axis=-1, keepdims=True)
        o = jnp.dot(p.astype(BF16), vm_ref[0, :, sl], preferred_element_type=F32)
        om_ref[0, :, sl] = (o / den).astype(BF16)


def _project(x3, norm_gain, w_in, b_forget, moba_q_gain, moba_k_gain, fox_q_gain, fox_k_gain,
             cos, sin, pool_mix, pool_scale, k_m, v_m, mem_q_gain):
    b, s, d = x3.shape
    width = MOBA_HEADS * HEAD_DIM
    pool_width = POOL_GROUP * len(POOL_WINDOWS)
    mem_width = MEM_HEADS * MEM_HEAD_DIM
    m_len = k_m.shape[1]
    tm = min(TM_PROJ, s)
    nt = s // tm
    o0 = 6 * width
    w_qkv = w_in[:, :o0].astype(BF16)
    w_f = jnp.pad(w_in[:, o0:o0 + FOX_HEADS], ((0, 0), (0, LANES - FOX_HEADS))).astype(BF16)
    b_f = jnp.pad(b_forget, (0, LANES - FOX_HEADS))[None, :]
    o1 = o0 + FOX_HEADS
    w_pool = w_in[:, o1:o1 + pool_width].astype(BF16)
    o2 = o1 + pool_width
    w_mq = w_in[:, o2:o2 + mem_width].astype(BF16)
    gains = jnp.stack([jnp.tile(g, HEADS_PER_GROUP)
                       for g in (moba_q_gain, moba_k_gain, fox_q_gain, fox_k_gain)])

    tile = lambda w: pl.BlockSpec((1, tm, w), lambda i, j: (i, j, 0))
    act = lambda w: jax.ShapeDtypeStruct((b, s, w), BF16)
    outs = pl.pallas_call(
        _proj_kernel,
        out_shape=(act(width),) * 6 + (jax.ShapeDtypeStruct((b, s, LANES), F32), act(pool_width),
                                       act(mem_width), jax.ShapeDtypeStruct((b, nt, 8, width), F32)),
        grid=(b, nt),
        in_specs=[tile(d), _resident((1, d)), _resident((d, o0)), _resident((d, LANES)),
                  _resident((d, pool_width)), _resident((d, mem_width)), _resident((1, LANES)),
                  _resident((4, LANES)), tile(LANES), tile(LANES),
                  _resident((len(POOL_WINDOWS), POOL_GROUP, POOL_GROUP)), _resident((1, pool_width)),
                  pl.BlockSpec((1, m_len, mem_width), lambda i, j: (i, 0, 0)),
                  pl.BlockSpec((1, m_len, mem_width), lambda i, j: (i, 0, 0)),
                  _resident((1, MEM_HEAD_DIM))],
        out_specs=(tile(width),) * 6 + (tile(LANES), tile(pool_width), tile(mem_width),
                                        pl.BlockSpec((1, 1, 8, width), lambda i, j: (i, j, 0, 0))),
        scratch_shapes=[pltpu.VMEM((POOL_HALO, pool_width), F32), pltpu.VMEM((1, LANES), F32)],
        compiler_params=pltpu.CompilerParams(dimension_semantics=("arbitrary", "arbitrary"),
                                             vmem_limit_bytes=VMEM_LIMIT),
        name="mixer_projection",
    )(x3, norm_gain[None, :], w_qkv, w_f, w_pool, w_mq, b_f, gains, cos, sin,
      pool_mix.astype(BF16), pool_scale[None, :], k_m, v_m, mem_q_gain[None, :])
    qa, ka, va, qc, kc, vc, c, ob, om, kmean = outs
    kmean = kmean[:, :, :tm // MOBA_BLOCK, :].reshape(b, s // MOBA_BLOCK, width)
    return qa, ka, va, qc, kc, vc, c, ob, om, kmean


def _softmax_step(t, row_bias, vb, m, l, acc):
    m_new = jnp.maximum(m, jnp.max(t, axis=-1, keepdims=True) + row_bias)
    alpha = jnp.exp(m - m_new)
    p = jnp.exp(t + (row_bias - m_new))
    l_new = alpha * l + jnp.sum(p, axis=-1, keepdims=True)
    acc_new = alpha * acc + jnp.dot(p.astype(BF16), vb, preferred_element_type=F32)
    return m_new, l_new, acc_new


def _head_mask(shape, hh):
    lane = lax.broadcasted_iota(jnp.int32, shape, 1)
    return (lane >= hh * HEAD_DIM) & (lane < (hh + 1) * HEAD_DIM)


def _fox_kernel(q_ref, k_ref, v_ref, ccol_ref, crow_ref, o_ref):
    hp = pl.program_id(1)
    i = pl.program_id(2)
    tq = q_ref.shape[1]
    q = q_ref[0]
    ccol = ccol_ref[0]
    lane = lax.broadcasted_iota(jnp.int32, ccol.shape, 1)
    rowi = lax.broadcasted_iota(jnp.int32, (tq, tq), 0)
    coli = lax.broadcasted_iota(jnp.int32, (tq, tq), 1)
    outs = []
    for hh in range(HEADS_PER_GROUP):
        qh = jnp.where(_head_mask(q.shape, hh), q, jnp.zeros_like(q))
        head = hp * HEADS_PER_GROUP + hh
        cq = jnp.sum(jnp.where(lane == head, ccol, 0.0), axis=-1, keepdims=True)

        def block(jb, carry, masked):
            k0 = pl.multiple_of(jb * tq, tq)
            kb = k_ref[0, pl.ds(k0, tq), :]
            vb = v_ref[0, pl.ds(k0, tq), :]
            t = lax.dot_general(qh, kb, _NT, preferred_element_type=F32)
            t = t - crow_ref[0, 0, hh:hh + 1, pl.ds(k0, tq)]
            if masked:
                t = jnp.where(coli <= rowi, t, NEG_INF)
            return _softmax_step(t, cq, vb, *carry)

        init = (jnp.full((tq, 1), NEG_INF, F32), jnp.zeros((tq, 1), F32), jnp.zeros((tq, LANES), F32))
        carry = lax.fori_loop(0, i, functools.partial(block, masked=False), init)
        m, l, acc = block(i, carry, masked=True)
        outs.append(acc / l)
    o_ref[0] = jnp.where(_head_mask(outs[0].shape, 0), outs[0], outs[1]).astype(o_ref.dtype)


def _fox_attention(q, k, v, c):
    b, s, width = q.shape
    n_grp = width // LANES
    tq = min(TQ_FOX, s)
    crow = c[:, :, :FOX_HEADS].transpose(0, 2, 1).reshape(b, n_grp, HEADS_PER_GROUP, s)
    return pl.pallas_call(
        _fox_kernel,
        out_shape=jax.ShapeDtypeStruct((b, s, width), BF16),
        grid=(b, n_grp, s // tq),
        in_specs=[pl.BlockSpec((1, tq, LANES), lambda bi, g, i: (bi, i, g)),
                  pl.BlockSpec((1, s, LANES), lambda bi, g, i: (bi, 0, g)),
                  pl.BlockSpec((1, s, LANES), lambda bi, g, i: (bi, 0, g)),
                  pl.BlockSpec((1, tq, LANES), lambda bi, g, i: (bi, i, 0)),
                  pl.BlockSpec((1, 1, HEADS_PER_GROUP, s), lambda bi, g, i: (bi, g, 0, 0))],
        out_specs=pl.BlockSpec((1, tq, LANES), lambda bi, g, i: (bi, i, g)),
        compiler_params=_params(),
        name="forgetting_attention",
    )(q, k, v, c, crow)


def _moba_kernel(q_ref, k_ref, v_ref, kmean_ref, o_ref):
    i = pl.program_id(2)
    tq = q_ref.shape[1]
    q = q_ref[0]
    kmean = kmean_ref[0]
    km_hi = kmean.astype(BF16)
    km_lo = (kmean - km_hi.astype(F32)).astype(BF16)
    lane = lax.broadcasted_iota(jnp.int32, (tq, LANES), 1)
    lane_f = lane.astype(F32)
    rowi = lax.broadcasted_iota(jnp.int32, (tq, tq), 0)
    coli = lax.broadcasted_iota(jnp.int32, (tq, tq), 1)
    zero_bias = jnp.zeros((tq, 1), F32)
    outs = []
    for hh in range(HEADS_PER_GROUP):
        qh = jnp.where(_head_mask(q.shape, hh), q, jnp.zeros_like(q))
        gate = (lax.dot_general(qh, km_hi, _NT, preferred_element_type=F32)
                + lax.dot_general(qh, km_lo, _NT, preferred_element_type=F32))
        past = lane < i
        gate = jnp.where(past, gate, NEG_INF)
        sel = jnp.zeros((tq, LANES), F32)
        for _ in range(MOBA_TOPK):
            top = jnp.max(gate, axis=-1, keepdims=True)
            idx = jnp.min(jnp.where(gate == top, lane_f, float(LANES)), axis=-1, keepdims=True)
            pick = lane_f == idx
            sel = jnp.where(pick & past, 1.0, sel)
            gate = jnp.where(pick, -jnp.inf, gate)

        def block(jb, carry, own):
            k0 = pl.multiple_of(jb * tq, tq)
            kb = k_ref[0, pl.ds(k0, tq), :]
            vb = v_ref[0, pl.ds(k0, tq), :]
            t = lax.dot_general(qh, kb, _NT, preferred_element_type=F32)
            if own:
                t = jnp.where(coli <= rowi, t, NEG_INF)
            else:
                chosen = jnp.max(jnp.where(lane == jb, sel, 0.0), axis=-1, keepdims=True)
                t = jnp.where(chosen > 0.0, t, NEG_INF)
            return _softmax_step(t, zero_bias, vb, *carry)

        init = (jnp.full((tq, 1), NEG_INF, F32), jnp.zeros((tq, 1), F32), jnp.zeros((tq, LANES), F32))
        carry = block(i, init, own=True)
        m, l, acc = lax.fori_loop(0, i, functools.partial(block, own=False), carry)
        outs.append(acc / l)
    o_ref[0] = jnp.where(_head_mask(outs[0].shape, 0), outs[0], outs[1]).astype(o_ref.dtype)


def _moba_attention(q, k, v, kmean):
    b, s, width = q.shape
    n_grp = width // LANES
    tq = MOBA_BLOCK
    n_blk = s // MOBA_BLOCK
    kmean = jnp.pad(kmean, ((0, 0), (0, LANES - n_blk), (0, 0)))
    return pl.pallas_call(
        _moba_kernel,
        out_shape=jax.ShapeDtypeStruct((b, s, width), BF16),
        grid=(b, n_grp, s // tq),
        in_specs=[pl.BlockSpec((1, tq, LANES), lambda bi, g, i: (bi, i, g)),
                  pl.BlockSpec((1, s, LANES), lambda bi, g, i: (bi, 0, g)),
                  pl.BlockSpec((1, s, LANES), lambda bi, g, i: (bi, 0, g)),
                  pl.BlockSpec((1, LANES, LANES), lambda bi, g, i: (bi, 0, g))],
        out_specs=pl.BlockSpec((1, tq, LANES), lambda bi, g, i: (bi, i, g)),
        compiler_params=_params(),
        name="moba_attention",
    )(q, k, v, kmean)


def _merge_kernel(x_ref, gn_ref, oa_ref, ob_ref, oc_ref, om_ref, wg_ref, wbr_ref, wo_ref, o_ref):
    x = x_ref[...]
    d = x.shape[1]
    h = _rms(x, gn_ref[...]).astype(BF16)
    mix = jnp.zeros_like(x)
    for n, br in enumerate((oa_ref, ob_ref, oc_ref, om_ref)):
        gate = jax.nn.sigmoid(jnp.dot(h, wg_ref[:, n * d:(n + 1) * d], preferred_element_type=F32))
        y = jnp.dot(br[...], wbr_ref[n], preferred_element_type=F32)
        mix = mix + gate * y
    o_ref[...] = x + jnp.dot(mix.astype(BF16), wo_ref[...], preferred_element_type=F32)


def _merge(x2, norm_gain, branches, w_gate, w_branch, w_out):
    t, d = x2.shape
    bw = w_branch.shape[1]
    tm = min(TM_FFN, t)
    row = lambda w: pl.BlockSpec((tm, w), lambda i: (i, 0))
    return pl.pallas_call(
        _merge_kernel,
        out_shape=jax.ShapeDtypeStruct((t, d), F32),
        grid=(t // tm,),
        in_specs=[row(d), _resident((1, d)), row(bw), row(bw), row(bw), row(bw),
                  _resident((d, N_BRANCH * d)), _resident((N_BRANCH, bw, d)), _resident((d, d))],
        out_specs=row(d),
        compiler_params=_params(),
        name="gated_merge",
    )(x2, norm_gain[None, :], *branches, w_gate.astype(BF16), w_branch.astype(BF16), w_out.astype(BF16))


def kernel(x, mem, positions, norm_ffn1, w_ffn1_in, w_ffn1_out, norm_mix, w_in, b_forget, moba_q_gain, moba_k_gain, fox_q_gain, fox_k_gain, norm_mem, w_mem_kv, mem_q_gain, mem_k_gain, pool_mix, pool_scale, w_branch, w_out, norm_ffn2, w_ffn2_in, w_ffn2_out):
    b, s, d = x.shape
    depth = norm_ffn1.shape[0]
    gate_off = w_in.shape[2] - N_BRANCH * d
    cos, sin = _rope_tables(positions)
    x2 = x.reshape(b * s, d)
    for l in range(depth):
        x2 = _ffn(x2, norm_ffn1[l], w_ffn1_in[l], w_ffn1_out[l])
        k_m, v_m = _mem_kv(mem, norm_mem[l], w_mem_kv[l], mem_k_gain[l])
        qa, ka, va, qc, kc, vc, c, ob, om, kmean = _project(
            x2.reshape(b, s, d), norm_mix[l], w_in[l, :, :gate_off], b_forget[l],
            moba_q_gain[l], moba_k_gain[l], fox_q_gain[l], fox_k_gain[l],
            cos, sin, pool_mix[l], pool_scale[l], k_m, v_m, mem_q_gain[l])
        oa = _moba_attention(qa, ka, va, kmean)
        oc = _fox_attention(qc, kc, vc, c)
        flat = lambda a: a.reshape(b * s, a.shape[2])
        x2 = _merge(x2, norm_mix[l], (flat(oa), flat(ob), flat(oc), flat(om)),
                    w_in[l, :, gate_off:], w_branch[l], w_out[l])
        x2 = _ffn(x2, norm_ffn2[l], w_ffn2_in[l], w_ffn2_out[l])
    return x2.reshape(b, s, d)
```

```python
import functools

import jax
import jax.numpy as jnp
from jax import lax
from jax.experimental import pallas as pl
from jax.experimental.pallas import tpu as pltpu

HEAD_DIM = 64
MOBA_HEADS = 8
MOBA_BLOCK = 256
MOBA_TOPK = 3
FOX_HEADS = 8
POOL_WINDOWS = (2, 4, 8, 16)
POOL_GROUP = 128
MEM_HEADS = 4
MEM_HEAD_DIM = 128
ROPE_THETA = 500000.0
ROT_DIM = HEAD_DIM // 4
N_BRANCH = 4
RMS_EPS = 1e-6
NEG_INF = -1e30

LANES = 128
HEADS_PER_GROUP = LANES // HEAD_DIM
POOL_HALO = 16
VMEM_LIMIT = 56 * 1024 * 1024

TM_FFN = 512
TM_PROJ = 512
TQ_ATTN = 512
TK_ATTN = 256
AUG_ROWS = 16
FFN_CHUNK = 256

F32 = jnp.float32
BF16 = jnp.bfloat16
_NT = (((1,), (1,)), ((), ()))


def _params():
    return pltpu.CompilerParams(vmem_limit_bytes=VMEM_LIMIT)


def _resident(shape):
    nd = len(shape)
    return pl.BlockSpec(shape, lambda *_: (0,) * nd, pipeline_mode=pl.Buffered(1))


def _rms(x, gain):
    ms = jnp.mean(x * x, axis=-1, keepdims=True)
    return x * lax.rsqrt(ms + RMS_EPS) * gain


def _head_pair_rms(x, gain):
    lane = lax.broadcasted_iota(jnp.int32, x.shape, 1)
    lo = lane < HEAD_DIM
    sq = x * x
    s_lo = jnp.sum(jnp.where(lo, sq, 0.0), axis=-1, keepdims=True)
    s_hi = jnp.sum(jnp.where(lo, 0.0, sq), axis=-1, keepdims=True)
    ms = jnp.where(lo, s_lo, s_hi) * (1.0 / HEAD_DIM)
    return x * lax.rsqrt(ms + RMS_EPS) * gain


def _rotate(x, cos, sin):
    lane = lax.broadcasted_iota(jnp.int32, x.shape, 1)
    half = ROT_DIM // 2
    fwd = pltpu.roll(x, LANES - half, axis=1)
    bwd = pltpu.roll(x, half, axis=1)
    partner = jnp.where((lane % HEAD_DIM) < half, fwd, bwd)
    return x * cos + partner * sin


def _rope_kernel(pos_ref, freq_ref, sign_ref, cos_ref, sin_ref):
    ang = pos_ref[0].astype(F32) * freq_ref[...]
    cos_ref[0] = jnp.cos(ang)
    sin_ref[0] = jnp.sin(ang) * sign_ref[...]


def _rope_tables(positions):
    b, s = positions.shape
    half = ROT_DIM // 2
    inv_freq = jnp.power(ROPE_THETA, -jnp.arange(half, dtype=F32) * (2.0 / ROT_DIM))
    lane = jnp.arange(LANES) % HEAD_DIM
    freq = jnp.where(lane < ROT_DIM, inv_freq[lane % half], 0.0).astype(F32)[None, :]
    sign = jnp.where(lane < half, -1.0, jnp.where(lane < ROT_DIM, 1.0, 0.0)).astype(F32)[None, :]
    ts = min(s, 1024)
    return pl.pallas_call(
        _rope_kernel,
        out_shape=(jax.ShapeDtypeStruct((b, s, LANES), F32),) * 2,
        grid=(b, s // ts),
        in_specs=[pl.BlockSpec((1, ts, 1), lambda i, j: (i, j, 0)),
                  pl.BlockSpec((1, LANES), lambda i, j: (0, 0)),
                  pl.BlockSpec((1, LANES), lambda i, j: (0, 0))],
        out_specs=(pl.BlockSpec((1, ts, LANES), lambda i, j: (i, j, 0)),) * 2,
        compiler_params=_params(),
        name="rope_tables",
    )(positions.reshape(b, s, 1), freq, sign)


def _ffn_kernel(x_ref, g_ref, wg_ref, wu_ref, wo_ref, o_ref):
    x = x_ref[...]
    h = _rms(x, g_ref[...]).astype(BF16)
    d_ff = wg_ref.shape[1]
    acc = jnp.zeros_like(x)
    for c in range(d_ff // FFN_CHUNK):
        sl = slice(c * FFN_CHUNK, (c + 1) * FFN_CHUNK)
        g = jnp.dot(h, wg_ref[:, sl], preferred_element_type=F32)
        u = jnp.dot(h, wu_ref[:, sl], preferred_element_type=F32)
        a = (g * jax.nn.sigmoid(g) * u).astype(BF16)
        acc = acc + jnp.dot(a, wo_ref[sl, :], preferred_element_type=F32)
    o_ref[...] = x + 0.5 * acc


def _ffn(x2, gain, w_in, w_out):
    t, d = x2.shape
    d_ff = w_out.shape[0]
    wg = w_in[:, :d_ff].astype(BF16)
    wu = w_in[:, d_ff:].astype(BF16)
    wo = w_out.astype(BF16)
    tm = min(TM_FFN, t)
    return pl.pallas_call(
        _ffn_kernel,
        out_shape=jax.ShapeDtypeStruct((t, d), F32),
        grid=(t // tm,),
        in_specs=[pl.BlockSpec((tm, d), lambda i: (i, 0)),
                  _resident((1, d)), _resident((d, d_ff)), _resident((d, d_ff)), _resident((d_ff, d))],
        out_specs=pl.BlockSpec((tm, d), lambda i: (i, 0)),
        compiler_params=_params(),
        name="ffn_half_step",
    )(x2, gain[None, :], wg, wu, wo)


def _memkv_kernel(mem_ref, gn_ref, w_ref, kg_ref, k_ref, v_ref):
    width = k_ref.shape[2]
    mn = _rms(mem_ref[0], gn_ref[...]).astype(BF16)
    kv = jnp.dot(mn, w_ref[...], preferred_element_type=F32)
    for h in range(MEM_HEADS):
        sl = slice(h * MEM_HEAD_DIM, (h + 1) * MEM_HEAD_DIM)
        k_ref[0, :, sl] = _rms(kv[:, sl], kg_ref[...]).astype(BF16)
    v_ref[0] = kv[:, width:].astype(BF16)


def _mem_kv(mem, norm_gain, w_kv, k_gain):
    b, m, d = mem.shape
    width = MEM_HEADS * MEM_HEAD_DIM
    return pl.pallas_call(
        _memkv_kernel,
        out_shape=(jax.ShapeDtypeStruct((b, m, width), BF16),) * 2,
        grid=(b,),
        in_specs=[pl.BlockSpec((1, m, d), lambda i: (i, 0, 0)),
                  _resident((1, d)), _resident((d, 2 * width)), _resident((1, MEM_HEAD_DIM))],
        out_specs=(pl.BlockSpec((1, m, width), lambda i: (i, 0, 0)),) * 2,
        compiler_params=_params(),
        name="memory_kv",
    )(mem, norm_gain[None, :], w_kv.astype(BF16), k_gain[None, :])


def _proj_kernel(x_ref, gn_ref, wqkv_ref, wf_ref, wp_ref, wmq_ref, bf_ref, gains_ref,
                 cos_ref, sin_ref, pmix_ref, pscale_ref, km_ref, vm_ref, mqg_ref,
                 qa_ref, ka_ref, va_ref, qc_ref, kc_ref, vc_ref, c_ref, ob_ref, om_ref, kmean_ref,
                 pool_carry, c_carry):
    j = pl.program_id(1)
    tm = x_ref.shape[1]
    width = qa_ref.shape[2]
    n_grp = width // LANES

    @pl.when(j == 0)
    def _():
        pool_carry[...] = jnp.zeros_like(pool_carry)
        c_carry[...] = jnp.zeros_like(c_carry)

    h = _rms(x_ref[0], gn_ref[...]).astype(BF16)
    scale = HEAD_DIM ** -0.5

    cos = cos_ref[0]
    sin = sin_ref[0]
    kmean_ref[...] = jnp.zeros_like(kmean_ref)
    for idx, (q_out, k_out, v_out, rot) in enumerate(((qa_ref, ka_ref, va_ref, True),
                                                      (qc_ref, kc_ref, vc_ref, False))):
        base = idx * 3 * width
        for g in range(n_grp):
            sl = slice(g * LANES, (g + 1) * LANES)
            q = jnp.dot(h, wqkv_ref[:, base + g * LANES: base + (g + 1) * LANES],
                        preferred_element_type=F32)
            k = jnp.dot(h, wqkv_ref[:, base + width + g * LANES: base + width + (g + 1) * LANES],
                        preferred_element_type=F32)
            v = jnp.dot(h, wqkv_ref[:, base + 2 * width + g * LANES: base + 2 * width + (g + 1) * LANES],
                        preferred_element_type=F32)
            q = _head_pair_rms(q, gains_ref[2 * idx: 2 * idx + 1, :])
            k = _head_pair_rms(k, gains_ref[2 * idx + 1: 2 * idx + 2, :])
            if rot:
                q = _rotate(q, cos, sin)
                k = _rotate(k, cos, sin)
                for sb in range(tm // MOBA_BLOCK):
                    kmean_ref[0, 0, sb:sb + 1, sl] = jnp.mean(
                        k[sb * MOBA_BLOCK:(sb + 1) * MOBA_BLOCK], axis=0, keepdims=True)
            q_out[0, :, sl] = (q * scale).astype(BF16)
            k_out[0, :, sl] = k.astype(BF16)
            v_out[0, :, sl] = v.astype(BF16)

    lf = jax.nn.log_sigmoid(jnp.dot(h, wf_ref[...], preferred_element_type=F32) + bf_ref[...])
    row = lax.broadcasted_iota(jnp.int32, lf.shape, 0)
    step = 1
    while step < tm:
        lf = lf + jnp.where(row >= step, pltpu.roll(lf, step, axis=0), 0.0)
        step *= 2
    c = lf + c_carry[...]
    c_ref[0] = c
    c_carry[...] = c[tm - 1:tm, :]

    u = jnp.dot(h, wp_ref[...], preferred_element_type=F32)
    ext = jnp.concatenate([pool_carry[...], u], axis=0)
    pos = j * tm + lax.broadcasted_iota(jnp.int32, (tm, 1), 0)
    for g, w in enumerate(POOL_WINDOWS):
        sl = slice(g * POOL_GROUP, (g + 1) * POOL_GROUP)
        win = ext[:, sl]
        step = 1
        while step < w:
            win = win + pltpu.roll(win, step, axis=0)
            step *= 2
        cnt = jnp.minimum(pos + 1, w).astype(F32)
        pooled = (win[POOL_HALO:, :] / cnt - u[:, sl]).astype(BF16)
        mixed = jnp.dot(pooled, pmix_ref[g], preferred_element_type=F32)
        ob_ref[0, :, sl] = (mixed * pscale_ref[:, sl]).astype(BF16)
    pool_carry[...] = u[tm - POOL_HALO:, :]

    for hd in range(MEM_HEADS):
        sl = slice(hd * MEM_HEAD_DIM, (hd + 1) * MEM_HEAD_DIM)
        qm = jnp.dot(h, wmq_ref[:, sl], preferred_element_type=F32)
        qm = _rms(qm, mqg_ref[...]).astype(BF16)
        s = lax.dot_general(qm, km_ref[0, :, sl], _NT, preferred_element_type=F32) * (MEM_HEAD_DIM ** -0.5)
        p = jnp.exp(s - jnp.max(s, axis=-1, keepdims=True))
        den = jnp.sum(p, axis=-1, keepdims=True)
        o = jnp.dot(p.astype(BF16), vm_ref[0, :, sl], preferred_element_type=F32)
        om_ref[0, :, sl] = (o / den).astype(BF16)


def _project(x3, norm_gain, w_in, b_forget, moba_q_gain, moba_k_gain, fox_q_gain, fox_k_gain,
             cos, sin, pool_mix, pool_scale, k_m, v_m, mem_q_gain):
    b, s, d = x3.shape
    width = MOBA_HEADS * HEAD_DIM
    pool_width = POOL_GROUP * len(POOL_WINDOWS)
    mem_width = MEM_HEADS * MEM_HEAD_DIM
    m_len = k_m.shape[1]
    tm = min(TM_PROJ, s)
    nt = s // tm
    o0 = 6 * width
    w_qkv = w_in[:, :o0].astype(BF16)
    w_f = jnp.pad(w_in[:, o0:o0 + FOX_HEADS], ((0, 0), (0, LANES - FOX_HEADS))).astype(BF16)
    b_f = jnp.pad(b_forget, (0, LANES - FOX_HEADS))[None, :]
    o1 = o0 + FOX_HEADS
    w_pool = w_in[:, o1:o1 + pool_width].astype(BF16)
    o2 = o1 + pool_width
    w_mq = w_in[:, o2:o2 + mem_width].astype(BF16)
    gains = jnp.stack([jnp.tile(g, HEADS_PER_GROUP)
                       for g in (moba_q_gain, moba_k_gain, fox_q_gain, fox_k_gain)])

    tile = lambda w: pl.BlockSpec((1, tm, w), lambda i, j: (i, j, 0))
    act = lambda w: jax.ShapeDtypeStruct((b, s, w), BF16)
    outs = pl.pallas_call(
        _proj_kernel,
        out_shape=(act(width),) * 6 + (jax.ShapeDtypeStruct((b, s, LANES), F32), act(pool_width),
                                       act(mem_width), jax.ShapeDtypeStruct((b, nt, 8, width), F32)),
        grid=(b, nt),
        in_specs=[tile(d), _resident((1, d)), _resident((d, o0)), _resident((d, LANES)),
                  _resident((d, pool_width)), _resident((d, mem_width)), _resident((1, LANES)),
                  _resident((4, LANES)), tile(LANES), tile(LANES),
                  _resident((len(POOL_WINDOWS), POOL_GROUP, POOL_GROUP)), _resident((1, pool_width)),
                  pl.BlockSpec((1, m_len, mem_width), lambda i, j: (i, 0, 0)),
                  pl.BlockSpec((1, m_len, mem_width), lambda i, j: (i, 0, 0)),
                  _resident((1, MEM_HEAD_DIM))],
        out_specs=(tile(width),) * 6 + (tile(LANES), tile(pool_width), tile(mem_width),
                                        pl.BlockSpec((1, 1, 8, width), lambda i, j: (i, j, 0, 0))),
        scratch_shapes=[pltpu.VMEM((POOL_HALO, pool_width), F32), pltpu.VMEM((1, LANES), F32)],
        compiler_params=pltpu.CompilerParams(dimension_semantics=("arbitrary", "arbitrary"),
                                             vmem_limit_bytes=VMEM_LIMIT),
        name="mixer_projection",
    )(x3, norm_gain[None, :], w_qkv, w_f, w_pool, w_mq, b_f, gains, cos, sin,
      pool_mix.astype(BF16), pool_scale[None, :], k_m, v_m, mem_q_gain[None, :])
    qa, ka, va, qc, kc, vc, c, ob, om, kmean = outs
    kmean = kmean[:, :, :tm // MOBA_BLOCK, :].reshape(b, s // MOBA_BLOCK, width)
    return qa, ka, va, qc, kc, vc, c, ob, om, kmean


def _stage_values(v_ref, vt_ref):
    vt_ref[...] = v_ref[0].astype(F32).T.astype(BF16)


def _stage_keys(k_ref, kaug_ref, r0, rows, ext0, ext1):
    k = k_ref[0, pl.ds(r0, rows), :]
    lane = lax.broadcasted_iota(jnp.int32, k.shape, 1)
    kaug_ref[0, pl.ds(r0, rows), :] = jnp.where(lane < HEAD_DIM, k, ext0.astype(BF16))
    kaug_ref[1, pl.ds(r0, rows), :] = jnp.where(lane >= HEAD_DIM, k, ext1.astype(BF16))


def _augment_queries(qt, e0, e1):
    tq = qt.shape[1]
    pad = jnp.zeros((HEAD_DIM - AUG_ROWS, tq), F32)
    qa0 = jnp.concatenate([qt[:HEAD_DIM], e0, pad], axis=0).astype(BF16)
    qa1 = jnp.concatenate([e1, pad, qt[HEAD_DIM:]], axis=0).astype(BF16)
    return qa0, qa1


def _attend_pair(qa, kaug_ref, vt_ref, s_ref, i, o_ref):
    tq = o_ref.shape[1]
    tk = TK_ATTN
    key_i = lax.broadcasted_iota(jnp.int32, (tk, tq), 0)
    qry_i = lax.broadcasted_iota(jnp.int32, (tk, tq), 1)

    def issue_scores(jb, slot):
        k0 = pl.multiple_of(jb * tk, tk)
        for hh in range(HEADS_PER_GROUP):
            s_ref[slot, hh] = jnp.dot(kaug_ref[hh, pl.ds(k0, tk), :], qa[hh],
                                      preferred_element_type=F32)

    def reduce_block(jb, slot, carry, diag=None):
        k0 = pl.multiple_of(jb * tk, tk)
        new = []
        for hh in range(HEADS_PER_GROUP):
            m, l, acc = carry[hh]
            s = s_ref[slot, hh]
            if diag is not None:
                s = jnp.where(key_i + diag * tk <= qry_i, s, NEG_INF)
            m_new = jnp.maximum(m, jnp.max(s, axis=0, keepdims=True))
            alpha = jnp.exp(m - m_new)
            p = jnp.exp(s - m_new)
            l_new = alpha * l + jnp.sum(p, axis=0, keepdims=True)
            vb = vt_ref[pl.ds(hh * HEAD_DIM, HEAD_DIM), pl.ds(k0, tk)]
            acc_new = alpha * acc + jnp.dot(vb, p.astype(BF16), preferred_element_type=F32)
            new.append((m_new, l_new, acc_new))
        return tuple(new)

    def step(t, carry):
        issue_scores(2 * t + 1, 1)
        carry = reduce_block(2 * t, 0, carry)
        issue_scores(2 * t + 2, 0)
        return reduce_block(2 * t + 1, 1, carry)

    init = tuple((jnp.full((1, tq), NEG_INF, F32), jnp.zeros((1, tq), F32), jnp.zeros((HEAD_DIM, tq), F32))
                 for _ in range(HEADS_PER_GROUP))
    issue_scores(0, 0)
    carry = lax.fori_loop(0, i, step, init)
    issue_scores(2 * i + 1, 1)
    carry = reduce_block(2 * i, 0, carry, diag=0)
    carry = reduce_block(2 * i + 1, 1, carry, diag=1)
    ot = jnp.concatenate([acc / l for (_, l, acc) in carry], axis=0)
    o_ref[0] = ot.T.astype(o_ref.dtype)


def _split3(x):
    hi = x.astype(BF16).astype(F32)
    r = x - hi
    mid = r.astype(BF16).astype(F32)
    return hi, mid, r - mid


def _fox_kernel(q_ref, k_ref, v_ref, c_ref, crow_ref, o_ref, kaug_ref, vt_ref, s_ref):
    g = pl.program_id(1)
    i = pl.program_id(2)
    tq = q_ref.shape[1]
    s_len = k_ref.shape[1]
    n_parts = 3

    @pl.when(i == 0)
    def _():
        _stage_values(v_ref, vt_ref)
        r = lax.broadcasted_iota(jnp.int32, (LANES, 2 * LANES), 0)
        col = lax.broadcasted_iota(jnp.int32, (LANES, 2 * LANES), 1)
        place = []
        for part in range(n_parts):
            hit = (((r == HEADS_PER_GROUP * g) & (col == HEAD_DIM + n_parts + part))
                   | ((r == HEADS_PER_GROUP * g + 1) & (col == LANES + n_parts + part)))
            place.append(jnp.where(hit, -1.0, 0.0).astype(BF16))
        lane2 = lax.broadcasted_iota(jnp.int32, (1, 2 * LANES), 1)
        ones = jnp.where(((lane2 >= HEAD_DIM) & (lane2 < HEAD_DIM + n_parts))
                         | ((lane2 >= LANES) & (lane2 < LANES + n_parts)), 1.0, 0.0)
        rows = min(512, s_len)

        def chunk(ci, _):
            r0 = pl.multiple_of(ci * rows, rows)
            parts = _split3(c_ref[0, pl.ds(r0, rows), :])
            ext = ones + sum(jnp.dot(p.astype(BF16), w, preferred_element_type=F32)
                             for p, w in zip(parts, place))
            _stage_keys(k_ref, kaug_ref, r0, rows, ext[:, :LANES], ext[:, LANES:])
            return 0

        lax.fori_loop(0, s_len // rows, chunk, 0)

    qt = q_ref[0].astype(F32).T
    row = lax.broadcasted_iota(jnp.int32, (AUG_ROWS, tq), 0)
    exts = []
    for hh in range(HEADS_PER_GROUP):
        hi, mid, lo = _split3(crow_ref[0, 0, hh:hh + 1, :])
        e = jnp.where(row == 0, hi, jnp.where(row == 1, mid, jnp.where(row == 2, lo, 0.0)))
        exts.append(jnp.where((row >= n_parts) & (row < 2 * n_parts), 1.0, e))
    _attend_pair(_augment_queries(qt, *exts), kaug_ref, vt_ref, s_ref, i, o_ref)


def _attention_call(kernel_fn, name, q, k, v, extra_inputs, extra_specs):
    b, s, width = q.shape
    n_grp = width // LANES
    tq = TQ_ATTN
    assert tq == 2 * TK_ATTN and s % tq == 0
    return pl.pallas_call(
        kernel_fn,
        out_shape=jax.ShapeDtypeStruct((b, s, width), BF16),
        grid=(b, n_grp, s // tq),
        in_specs=[pl.BlockSpec((1, tq, LANES), lambda bi, g, i: (bi, i, g)),
                  pl.BlockSpec((1, s, LANES), lambda bi, g, i: (bi, 0, g)),
                  pl.BlockSpec((1, s, LANES), lambda bi, g, i: (bi, 0, g))] + extra_specs,
        out_specs=pl.BlockSpec((1, tq, LANES), lambda bi, g, i: (bi, i, g)),
        scratch_shapes=[pltpu.VMEM((HEADS_PER_GROUP, s, LANES), BF16), pltpu.VMEM((LANES, s), BF16),
                        pltpu.VMEM((2, HEADS_PER_GROUP, TK_ATTN, tq), F32)],
        compiler_params=pltpu.CompilerParams(dimension_semantics=("arbitrary",) * 3,
                                             vmem_limit_bytes=VMEM_LIMIT),
        name=name,
    )(q, k, v, *extra_inputs)


def _fox_attention(q, k, v, c):
    b, s, width = q.shape
    n_grp = width // LANES
    tq = TQ_ATTN
    crow = c[:, :, :FOX_HEADS].transpose(0, 2, 1).reshape(b, n_grp, HEADS_PER_GROUP, s)
    return _attention_call(
        _fox_kernel, "forgetting_attention", q, k, v, (c, crow),
        [pl.BlockSpec((1, s, LANES), lambda bi, g, i: (bi, 0, 0)),
         pl.BlockSpec((1, 1, HEADS_PER_GROUP, tq), lambda bi, g, i: (bi, g, 0, i))])


def _moba_kernel(q_ref, k_ref, v_ref, kmean_ref, o_ref, kaug_ref, vt_ref, s_ref):
    i = pl.program_id(2)
    tq = q_ref.shape[1]
    s_len = k_ref.shape[1]

    @pl.when(i == 0)
    def _():
        _stage_values(v_ref, vt_ref)
        rows = min(512, s_len)

        def chunk(ci, _):
            r0 = pl.multiple_of(ci * rows, rows)
            blk = (r0 + lax.broadcasted_iota(jnp.int32, (rows, LANES), 0)) // MOBA_BLOCK
            lane = lax.broadcasted_iota(jnp.int32, (rows, LANES), 1)
            _stage_keys(k_ref, kaug_ref, r0, rows,
                        jnp.where(lane - HEAD_DIM == blk, 1.0, 0.0), jnp.where(lane == blk, 1.0, 0.0))
            return 0

        lax.fori_loop(0, s_len // rows, chunk, 0)

    qt = q_ref[0].astype(F32).T
    kmean = kmean_ref[0]
    km_hi = kmean.astype(BF16)
    km_lo = (kmean - km_hi.astype(F32)).astype(BF16)
    zeros = jnp.zeros((HEAD_DIM, tq), F32)
    row = lax.broadcasted_iota(jnp.int32, (AUG_ROWS, tq), 0)
    row_f = row.astype(F32)
    own = (i * tq + lax.broadcasted_iota(jnp.int32, (AUG_ROWS, tq), 1)) // MOBA_BLOCK
    past = row < own
    exts = []
    for hh in range(HEADS_PER_GROUP):
        qz = (jnp.concatenate([qt[:HEAD_DIM], zeros], axis=0) if hh == 0
              else jnp.concatenate([zeros, qt[HEAD_DIM:]], axis=0)).astype(BF16)
        gate = (jnp.dot(km_hi, qz, preferred_element_type=F32)
                + jnp.dot(km_lo, qz, preferred_element_type=F32))
        gate = jnp.where(past, gate, NEG_INF)
        keep = row == own
        for _ in range(MOBA_TOPK):
            top = jnp.max(gate, axis=0, keepdims=True)
            idx = jnp.min(jnp.where(gate == top, row_f, float(AUG_ROWS)), axis=0, keepdims=True)
            pick = row_f == idx
            keep = keep | (pick & past)
            gate = jnp.where(pick, -jnp.inf, gate)
        exts.append(jnp.where(keep, 0.0, NEG_INF))
    _attend_pair(_augment_queries(qt, *exts), kaug_ref, vt_ref, s_ref, i, o_ref)


def _moba_attention(q, k, v, kmean):
    b, s, width = q.shape
    n_blk = s // MOBA_BLOCK
    assert n_blk <= AUG_ROWS and TK_ATTN == MOBA_BLOCK
    kmean = jnp.pad(kmean, ((0, 0), (0, AUG_ROWS - n_blk), (0, 0)))
    return _attention_call(
        _moba_kernel, "moba_attention", q, k, v, (kmean,),
        [pl.BlockSpec((1, AUG_ROWS, LANES), lambda bi, g, i: (bi, 0, g))])


def _merge_kernel(x_ref, gn_ref, oa_ref, ob_ref, oc_ref, om_ref, wg_ref, wbr_ref, wo_ref, o_ref):
    x = x_ref[...]
    d = x.shape[1]
    h = _rms(x, gn_ref[...]).astype(BF16)
    mix = jnp.zeros_like(x)
    for n, br in enumerate((oa_ref, ob_ref, oc_ref, om_ref)):
        gate = jax.nn.sigmoid(jnp.dot(h, wg_ref[:, n * d:(n + 1) * d], preferred_element_type=F32))
        y = jnp.dot(br[...], wbr_ref[n], preferred_element_type=F32)
        mix = mix + gate * y
    o_ref[...] = x + jnp.dot(mix.astype(BF16), wo_ref[...], preferred_element_type=F32)


def _merge(x2, norm_gain, branches, w_gate, w_branch, w_out):
    t, d = x2.shape
    bw = w_branch.shape[1]
    tm = min(TM_FFN, t)
    row = lambda w: pl.BlockSpec((tm, w), lambda i: (i, 0))
    return pl.pallas_call(
        _merge_kernel,
        out_shape=jax.ShapeDtypeStruct((t, d), F32),
        grid=(t // tm,),
        in_specs=[row(d), _resident((1, d)), row(bw), row(bw), row(bw), row(bw),
                  _resident((d, N_BRANCH * d)), _resident((N_BRANCH, bw, d)), _resident((d, d))],
        out_specs=row(d),
        compiler_params=_params(),
        name="gated_merge",
    )(x2, norm_gain[None, :], *branches, w_gate.astype(BF16), w_branch.astype(BF16), w_out.astype(BF16))


def kernel(x, mem, positions, norm_ffn1, w_ffn1_in, w_ffn1_out, norm_mix, w_in, b_forget, moba_q_gain, moba_k_gain, fox_q_gain, fox_k_gain, norm_mem, w_mem_kv, mem_q_gain, mem_k_gain, pool_mix, pool_scale, w_branch, w_out, norm_ffn2, w_ffn2_in, w_ffn2_out):
    b, s, d = x.shape
    depth = norm_ffn1.shape[0]
    gate_off = w_in.shape[2] - N_BRANCH * d
    cos, sin = _rope_tables(positions)
    x2 = x.reshape(b * s, d)
    for l in range(depth):
        x2 = _ffn(x2, norm_ffn1[l], w_ffn1_in[l], w_ffn1_out[l])
        k_m, v_m = _mem_kv(mem, norm_mem[l], w_mem_kv[l], mem_k_gain[l])
        qa, ka, va, qc, kc, vc, c, ob, om, kmean = _project(
            x2.reshape(b, s, d), norm_mix[l], w_in[l, :, :gate_off], b_forget[l],
            moba_q_gain[l], moba_k_gain[l], fox_q_gain[l], fox_k_gain[l],
            cos, sin, pool_mix[l], pool_scale[l], k_m, v_m, mem_q_gain[l])
        oa = _moba_attention(qa, ka, va, kmean)
        oc = _fox_attention(qc, kc, vc, c)
        flat = lambda a: a.reshape(b * s, a.shape[2])
        x2 = _merge(x2, norm_mix[l], (flat(oa), flat(ob), flat(oc), flat(om)),
                    w_in[l, :, gate_off:], w_branch[l], w_out[l])
        x2 = _ffn(x2, norm_ffn2[l], w_ffn2_in[l], w_ffn2_out[l])
    return x2.reshape(b, s, d)
```

```python
import functools

import jax
import jax.numpy as jnp
from jax import lax
from jax.experimental import pallas as pl
from jax.experimental.pallas import tpu as pltpu

HEAD_DIM = 64
MOBA_HEADS = 8
MOBA_BLOCK = 256
MOBA_TOPK = 3
FOX_HEADS = 8
POOL_WINDOWS = (2, 4, 8, 16)
POOL_GROUP = 128
MEM_HEADS = 4
MEM_HEAD_DIM = 128
ROPE_THETA = 500000.0
ROT_DIM = HEAD_DIM // 4
N_BRANCH = 4
RMS_EPS = 1e-6
NEG_INF = -1e30

LANES = 128
HEADS_PER_GROUP = LANES // HEAD_DIM
POOL_HALO = 16
VMEM_LIMIT = 56 * 1024 * 1024

TM_FFN = 512
TM_PROJ = 512
TQ_ATTN = 512
TK_ATTN = 256
AUG_ROWS = 16
FFN_CHUNK = 256

LOG2E = 1.4426950408889634
V_ROWS = HEAD_DIM + AUG_ROWS

F32 = jnp.float32
BF16 = jnp.bfloat16
_NT = (((1,), (1,)), ((), ()))


def _params():
    return pltpu.CompilerParams(vmem_limit_bytes=VMEM_LIMIT)


def _resident(shape):
    nd = len(shape)
    return pl.BlockSpec(shape, lambda *_: (0,) * nd, pipeline_mode=pl.Buffered(1))


def _rms(x, gain):
    ms = jnp.mean(x * x, axis=-1, keepdims=True)
    return x * lax.rsqrt(ms + RMS_EPS) * gain


def _head_pair_rms(x, gain):
    lane = lax.broadcasted_iota(jnp.int32, x.shape, 1)
    lo = lane < HEAD_DIM
    sq = x * x
    s_lo = jnp.sum(jnp.where(lo, sq, 0.0), axis=-1, keepdims=True)
    s_hi = jnp.sum(jnp.where(lo, 0.0, sq), axis=-1, keepdims=True)
    ms = jnp.where(lo, s_lo, s_hi) * (1.0 / HEAD_DIM)
    return x * lax.rsqrt(ms + RMS_EPS) * gain


def _rotate(x, cos, sin):
    lane = lax.broadcasted_iota(jnp.int32, x.shape, 1)
    half = ROT_DIM // 2
    fwd = pltpu.roll(x, LANES - half, axis=1)
    bwd = pltpu.roll(x, half, axis=1)
    partner = jnp.where((lane % HEAD_DIM) < half, fwd, bwd)
    return x * cos + partner * sin


def _rope_kernel(pos_ref, freq_ref, sign_ref, cos_ref, sin_ref):
    ang = pos_ref[0].astype(F32) * freq_ref[...]
    cos_ref[0] = jnp.cos(ang)
    sin_ref[0] = jnp.sin(ang) * sign_ref[...]


def _rope_tables(positions):
    b, s = positions.shape
    half = ROT_DIM // 2
    inv_freq = jnp.power(ROPE_THETA, -jnp.arange(half, dtype=F32) * (2.0 / ROT_DIM))
    lane = jnp.arange(LANES) % HEAD_DIM
    freq = jnp.where(lane < ROT_DIM, inv_freq[lane % half], 0.0).astype(F32)[None, :]
    sign = jnp.where(lane < half, -1.0, jnp.where(lane < ROT_DIM, 1.0, 0.0)).astype(F32)[None, :]
    ts = min(s, 1024)
    return pl.pallas_call(
        _rope_kernel,
        out_shape=(jax.ShapeDtypeStruct((b, s, LANES), F32),) * 2,
        grid=(b, s // ts),
        in_specs=[pl.BlockSpec((1, ts, 1), lambda i, j: (i, j, 0)),
                  pl.BlockSpec((1, LANES), lambda i, j: (0, 0)),
                  pl.BlockSpec((1, LANES), lambda i, j: (0, 0))],
        out_specs=(pl.BlockSpec((1, ts, LANES), lambda i, j: (i, j, 0)),) * 2,
        compiler_params=_params(),
        name="rope_tables",
    )(positions.reshape(b, s, 1), freq, sign)


def _ffn_kernel(x_ref, g_ref, wg_ref, wu_ref, wo_ref, o_ref):
    x = x_ref[...]
    h = _rms(x, g_ref[...]).astype(BF16)
    d_ff = wg_ref.shape[1]
    acc = jnp.zeros_like(x)
    for c in range(d_ff // FFN_CHUNK):
        sl = slice(c * FFN_CHUNK, (c + 1) * FFN_CHUNK)
        g = jnp.dot(h, wg_ref[:, sl], preferred_element_type=F32)
        u = jnp.dot(h, wu_ref[:, sl], preferred_element_type=F32)
        a = (g * jax.nn.sigmoid(g) * u).astype(BF16)
        acc = acc + jnp.dot(a, wo_ref[sl, :], preferred_element_type=F32)
    o_ref[...] = x + 0.5 * acc


def _ffn(x2, gain, w_in, w_out):
    t, d = x2.shape
    d_ff = w_out.shape[0]
    wg = w_in[:, :d_ff].astype(BF16)
    wu = w_in[:, d_ff:].astype(BF16)
    wo = w_out.astype(BF16)
    tm = min(TM_FFN, t)
    return pl.pallas_call(
        _ffn_kernel,
        out_shape=jax.ShapeDtypeStruct((t, d), F32),
        grid=(t // tm,),
        in_specs=[pl.BlockSpec((tm, d), lambda i: (i, 0)),
                  _resident((1, d)), _resident((d, d_ff)), _resident((d, d_ff)), _resident((d_ff, d))],
        out_specs=pl.BlockSpec((tm, d), lambda i: (i, 0)),
        compiler_params=_params(),
        name="ffn_half_step",
    )(x2, gain[None, :], wg, wu, wo)


def _memkv_kernel(mem_ref, gn_ref, w_ref, kg_ref, k_ref, v_ref):
    width = k_ref.shape[2]
    mn = _rms(mem_ref[0], gn_ref[...]).astype(BF16)
    kv = jnp.dot(mn, w_ref[...], preferred_element_type=F32)
    for h in range(MEM_HEADS):
        sl = slice(h * MEM_HEAD_DIM, (h + 1) * MEM_HEAD_DIM)
        k_ref[0, :, sl] = _rms(kv[:, sl], kg_ref[...]).astype(BF16)
    v_ref[0] = kv[:, width:].astype(BF16)


def _mem_kv(mem, norm_gain, w_kv, k_gain):
    b, m, d = mem.shape
    width = MEM_HEADS * MEM_HEAD_DIM
    return pl.pallas_call(
        _memkv_kernel,
        out_shape=(jax.ShapeDtypeStruct((b, m, width), BF16),) * 2,
        grid=(b,),
        in_specs=[pl.BlockSpec((1, m, d), lambda i: (i, 0, 0)),
                  _resident((1, d)), _resident((d, 2 * width)), _resident((1, MEM_HEAD_DIM))],
        out_specs=(pl.BlockSpec((1, m, width), lambda i: (i, 0, 0)),) * 2,
        compiler_params=_params(),
        name="memory_kv",
    )(mem, norm_gain[None, :], w_kv.astype(BF16), k_gain[None, :])


def _proj_kernel(x_ref, gn_ref, wqkv_ref, wf_ref, wp_ref, wmq_ref, bf_ref, gains_ref,
                 cos_ref, sin_ref, pmix_ref, pscale_ref, km_ref, vm_ref, mqg_ref,
                 qa_ref, ka_ref, va_ref, qc_ref, kc_ref, vc_ref, c_ref, ob_ref, om_ref, kmean_ref,
                 pool_carry, c_carry):
    j = pl.program_id(1)
    tm = x_ref.shape[1]
    width = qa_ref.shape[2]
    n_grp = width // LANES

    @pl.when(j == 0)
    def _():
        pool_carry[...] = jnp.zeros_like(pool_carry)
        c_carry[...] = jnp.zeros_like(c_carry)

    h = _rms(x_ref[0], gn_ref[...]).astype(BF16)
    scale = HEAD_DIM ** -0.5 * LOG2E

    cos = cos_ref[0]
    sin = sin_ref[0]
    kmean_ref[...] = jnp.zeros_like(kmean_ref)
    qkv = jnp.dot(h, wqkv_ref[...], preferred_element_type=F32)
    for idx, (q_out, k_out, v_out, rot) in enumerate(((qa_ref, ka_ref, va_ref, True),
                                                      (qc_ref, kc_ref, vc_ref, False))):
        base = idx * 3 * width
        for g in range(n_grp):
            sl = slice(g * LANES, (g + 1) * LANES)
            q = qkv[:, base + g * LANES: base + (g + 1) * LANES]
            k = qkv[:, base + width + g * LANES: base + width + (g + 1) * LANES]
            v = qkv[:, base + 2 * width + g * LANES: base + 2 * width + (g + 1) * LANES]
            q = _head_pair_rms(q, gains_ref[2 * idx: 2 * idx + 1, :])
            k = _head_pair_rms(k, gains_ref[2 * idx + 1: 2 * idx + 2, :])
            if rot:
                q = _rotate(q, cos, sin)
                k = _rotate(k, cos, sin)
                for sb in range(tm // MOBA_BLOCK):
                    kmean_ref[0, 0, sb:sb + 1, sl] = jnp.mean(
                        k[sb * MOBA_BLOCK:(sb + 1) * MOBA_BLOCK], axis=0, keepdims=True)
            q_out[0, :, sl] = (q * scale).astype(BF16)
            k_out[0, :, sl] = k.astype(BF16)
            v_out[0, :, sl] = v.astype(BF16)

    lf = jax.nn.log_sigmoid(jnp.dot(h, wf_ref[...], preferred_element_type=F32) + bf_ref[...])
    row = lax.broadcasted_iota(jnp.int32, lf.shape, 0)
    step = 1
    while step < tm:
        lf = lf + jnp.where(row >= step, pltpu.roll(lf, step, axis=0), 0.0)
        step *= 2
    c = lf + c_carry[...]
    c_ref[0] = c
    c_carry[...] = c[tm - 1:tm, :]

    u = jnp.dot(h, wp_ref[...], preferred_element_type=F32)
    ext = jnp.concatenate([pool_carry[...], u], axis=0)
    pos = j * tm + lax.broadcasted_iota(jnp.int32, (tm, 1), 0)
    for g, w in enumerate(POOL_WINDOWS):
        sl = slice(g * POOL_GROUP, (g + 1) * POOL_GROUP)
        win = ext[:, sl]
        step = 1
        while step < w:
            win = win + pltpu.roll(win, step, axis=0)
            step *= 2
        cnt = jnp.minimum(pos + 1, w).astype(F32)
        pooled = (win[POOL_HALO:, :] / cnt - u[:, sl]).astype(BF16)
        mixed = jnp.dot(pooled, pmix_ref[g], preferred_element_type=F32)
        ob_ref[0, :, sl] = (mixed * pscale_ref[:, sl]).astype(BF16)
    pool_carry[...] = u[tm - POOL_HALO:, :]

    qm_all = jnp.dot(h, wmq_ref[...], preferred_element_type=F32)
    for hd in range(MEM_HEADS):
        sl = slice(hd * MEM_HEAD_DIM, (hd + 1) * MEM_HEAD_DIM)
        qm = _rms(qm_all[:, sl], mqg_ref[...]).astype(BF16)
        s = lax.dot_general(qm, km_ref[0, :, sl], _NT, preferred_element_type=F32) * (MEM_HEAD_DIM ** -0.5)
        p = jnp.exp(s - jnp.max(s, axis=-1, keepdims=True))
        den = jnp.sum(p, axis=-1, keepdims=True)
        o = jnp.dot(p.astype(BF16), vm_ref[0, :, sl], preferred_element_type=F32)
        om_ref[0, :, sl] = (o / den).astype(BF16)


def _project(x3, norm_gain, w_in, b_forget, moba_q_gain, moba_k_gain, fox_q_gain, fox_k_gain,
             cos, sin, pool_mix, pool_scale, k_m, v_m, mem_q_gain):
    b, s, d = x3.shape
    width = MOBA_HEADS * HEAD_DIM
    pool_width = POOL_GROUP * len(POOL_WINDOWS)
    mem_width = MEM_HEADS * MEM_HEAD_DIM
    m_len = k_m.shape[1]
    tm = min(TM_PROJ, s)
    nt = s // tm
    o0 = 6 * width
    w_qkv = w_in[:, :o0].astype(BF16)
    w_f = jnp.pad(w_in[:, o0:o0 + FOX_HEADS], ((0, 0), (0, LANES - FOX_HEADS))).astype(BF16)
    b_f = jnp.pad(b_forget, (0, LANES - FOX_HEADS))[None, :]
    o1 = o0 + FOX_HEADS
    w_pool = w_in[:, o1:o1 + pool_width].astype(BF16)
    o2 = o1 + pool_width
    w_mq = w_in[:, o2:o2 + mem_width].astype(BF16)
    gains = jnp.stack([jnp.tile(g, HEADS_PER_GROUP)
                       for g in (moba_q_gain, moba_k_gain, fox_q_gain, fox_k_gain)])

    tile = lambda w: pl.BlockSpec((1, tm, w), lambda i, j: (i, j, 0))
    act = lambda w: jax.ShapeDtypeStruct((b, s, w), BF16)
    outs = pl.pallas_call(
        _proj_kernel,
        out_shape=(act(width),) * 6 + (jax.ShapeDtypeStruct((b, s, LANES), F32), act(pool_width),
                                       act(mem_width), jax.ShapeDtypeStruct((b, nt, 8, width), F32)),
        grid=(b, nt),
        in_specs=[tile(d), _resident((1, d)), _resident((d, o0)), _resident((d, LANES)),
                  _resident((d, pool_width)), _resident((d, mem_width)), _resident((1, LANES)),
                  _resident((4, LANES)), tile(LANES), tile(LANES),
                  _resident((len(POOL_WINDOWS), POOL_GROUP, POOL_GROUP)), _resident((1, pool_width)),
                  pl.BlockSpec((1, m_len, mem_width), lambda i, j: (i, 0, 0)),
                  pl.BlockSpec((1, m_len, mem_width), lambda i, j: (i, 0, 0)),
                  _resident((1, MEM_HEAD_DIM))],
        out_specs=(tile(width),) * 6 + (tile(LANES), tile(pool_width), tile(mem_width),
                                        pl.BlockSpec((1, 1, 8, width), lambda i, j: (i, j, 0, 0))),
        scratch_shapes=[pltpu.VMEM((POOL_HALO, pool_width), F32), pltpu.VMEM((1, LANES), F32)],
        compiler_params=pltpu.CompilerParams(dimension_semantics=("arbitrary", "arbitrary"),
                                             vmem_limit_bytes=VMEM_LIMIT),
        name="mixer_projection",
    )(x3, norm_gain[None, :], w_qkv, w_f, w_pool, w_mq, b_f, gains, cos, sin,
      pool_mix.astype(BF16), pool_scale[None, :], k_m, v_m, mem_q_gain[None, :])
    qa, ka, va, qc, kc, vc, c, ob, om, kmean = outs
    kmean = kmean[:, :, :tm // MOBA_BLOCK, :].reshape(b, s // MOBA_BLOCK, width)
    return qa, ka, va, qc, kc, vc, c, ob, om, kmean


def _stage_values(v_ref, vt_ref):
    vt = v_ref[0].astype(F32).T
    s_len = vt.shape[1]
    row = lax.broadcasted_iota(jnp.int32, (AUG_ROWS, s_len), 0)
    tail = jnp.where(row == 0, 1.0, 0.0)
    for hh in range(HEADS_PER_GROUP):
        vt_ref[hh] = jnp.concatenate([vt[hh * HEAD_DIM:(hh + 1) * HEAD_DIM], tail], axis=0).astype(BF16)


def _stage_keys(k_ref, kaug_ref, r0, rows, ext0, ext1):
    k = k_ref[0, pl.ds(r0, rows), :]
    lane = lax.broadcasted_iota(jnp.int32, k.shape, 1)
    kaug_ref[0, pl.ds(r0, rows), :] = jnp.where(lane < HEAD_DIM, k, ext0.astype(BF16))
    kaug_ref[1, pl.ds(r0, rows), :] = jnp.where(lane >= HEAD_DIM, k, ext1.astype(BF16))


def _augment_queries(qt, e0, e1):
    tq = qt.shape[1]
    pad = jnp.zeros((HEAD_DIM - AUG_ROWS, tq), F32)
    qa0 = jnp.concatenate([qt[:HEAD_DIM], e0, pad], axis=0).astype(BF16)
    qa1 = jnp.concatenate([e1, pad, qt[HEAD_DIM:]], axis=0).astype(BF16)
    return qa0, qa1


def _attend_pair(qa, kaug_ref, vt_ref, s_ref, i, o_ref):
    tq = o_ref.shape[1]
    tk = TK_ATTN
    key_i = lax.broadcasted_iota(jnp.int32, (tk, tq), 0)
    qry_i = lax.broadcasted_iota(jnp.int32, (tk, tq), 1)

    def issue_scores(jb, slot):
        k0 = pl.multiple_of(jb * tk, tk)
        for hh in range(HEADS_PER_GROUP):
            s_ref[slot, hh] = jnp.dot(kaug_ref[hh, pl.ds(k0, tk), :], qa[hh],
                                      preferred_element_type=F32)

    def reduce_block(jb, slot, carry, diag=None):
        k0 = pl.multiple_of(jb * tk, tk)
        new = []
        for hh in range(HEADS_PER_GROUP):
            m, acc = carry[hh]
            s = s_ref[slot, hh]
            if diag is not None:
                s = jnp.where(key_i + diag * tk <= qry_i, s, NEG_INF)
            m_new = jnp.maximum(m, jnp.max(s, axis=0, keepdims=True))
            alpha = jnp.exp2(m - m_new)
            p = jnp.exp2(s - m_new).astype(BF16)
            vb = vt_ref[hh, :, pl.ds(k0, tk)]
            new.append((m_new, alpha * acc + jnp.dot(vb, p, preferred_element_type=F32)))
        return tuple(new)

    def step(t, carry):
        issue_scores(2 * t + 1, 1)
        carry = reduce_block(2 * t, 0, carry)
        issue_scores(2 * t + 2, 0)
        return reduce_block(2 * t + 1, 1, carry)

    init = tuple((jnp.full((1, tq), NEG_INF, F32), jnp.zeros((V_ROWS, tq), F32))
                 for _ in range(HEADS_PER_GROUP))
    issue_scores(0, 0)
    carry = lax.fori_loop(0, i, step, init)
    issue_scores(2 * i + 1, 1)
    carry = reduce_block(2 * i, 0, carry, diag=0)
    carry = reduce_block(2 * i + 1, 1, carry, diag=1)
    ot = jnp.concatenate([acc[:HEAD_DIM] / acc[HEAD_DIM:HEAD_DIM + 1] for (_, acc) in carry], axis=0)
    o_ref[0] = ot.T.astype(o_ref.dtype)


def _split3(x):
    hi = x.astype(BF16).astype(F32)
    r = x - hi
    mid = r.astype(BF16).astype(F32)
    return hi, mid, r - mid


def _fox_kernel(q_ref, k_ref, v_ref, c_ref, crow_ref, o_ref, kaug_ref, vt_ref, s_ref):
    g = pl.program_id(1)
    i = pl.program_id(2)
    tq = q_ref.shape[1]
    s_len = k_ref.shape[1]
    n_parts = 3

    @pl.when(i == 0)
    def _():
        _stage_values(v_ref, vt_ref)
        r = lax.broadcasted_iota(jnp.int32, (LANES, 2 * LANES), 0)
        col = lax.broadcasted_iota(jnp.int32, (LANES, 2 * LANES), 1)
        place = []
        for part in range(n_parts):
            hit = (((r == HEADS_PER_GROUP * g) & (col == HEAD_DIM + n_parts + part))
                   | ((r == HEADS_PER_GROUP * g + 1) & (col == LANES + n_parts + part)))
            place.append(jnp.where(hit, -1.0, 0.0).astype(BF16))
        lane2 = lax.broadcasted_iota(jnp.int32, (1, 2 * LANES), 1)
        ones = jnp.where(((lane2 >= HEAD_DIM) & (lane2 < HEAD_DIM + n_parts))
                         | ((lane2 >= LANES) & (lane2 < LANES + n_parts)), 1.0, 0.0)
        rows = min(512, s_len)

        def chunk(ci, _):
            r0 = pl.multiple_of(ci * rows, rows)
            parts = _split3(c_ref[0, pl.ds(r0, rows), :] * LOG2E)
            ext = ones + sum(jnp.dot(p.astype(BF16), w, preferred_element_type=F32)
                             for p, w in zip(parts, place))
            _stage_keys(k_ref, kaug_ref, r0, rows, ext[:, :LANES], ext[:, LANES:])
            return 0

        lax.fori_loop(0, s_len // rows, chunk, 0)

    qt = q_ref[0].astype(F32).T
    row = lax.broadcasted_iota(jnp.int32, (AUG_ROWS, tq), 0)
    exts = []
    for hh in range(HEADS_PER_GROUP):
        hi, mid, lo = _split3(crow_ref[0, 0, hh:hh + 1, :] * LOG2E)
        e = jnp.where(row == 0, hi, jnp.where(row == 1, mid, jnp.where(row == 2, lo, 0.0)))
        exts.append(jnp.where((row >= n_parts) & (row < 2 * n_parts), 1.0, e))
    _attend_pair(_augment_queries(qt, *exts), kaug_ref, vt_ref, s_ref, i, o_ref)


def _attention_call(kernel_fn, name, q, k, v, extra_inputs, extra_specs):
    b, s, width = q.shape
    n_grp = width // LANES
    tq = TQ_ATTN
    assert tq == 2 * TK_ATTN and s % tq == 0
    return pl.pallas_call(
        kernel_fn,
        out_shape=jax.ShapeDtypeStruct((b, s, width), BF16),
        grid=(b, n_grp, s // tq),
        in_specs=[pl.BlockSpec((1, tq, LANES), lambda bi, g, i: (bi, i, g)),
                  pl.BlockSpec((1, s, LANES), lambda bi, g, i: (bi, 0, g)),
                  pl.BlockSpec((1, s, LANES), lambda bi, g, i: (bi, 0, g))] + extra_specs,
        out_specs=pl.BlockSpec((1, tq, LANES), lambda bi, g, i: (bi, i, g)),
        scratch_shapes=[pltpu.VMEM((HEADS_PER_GROUP, s, LANES), BF16), pltpu.VMEM((HEADS_PER_GROUP, V_ROWS, s), BF16),
                        pltpu.VMEM((2, HEADS_PER_GROUP, TK_ATTN, tq), F32)],
        compiler_params=pltpu.CompilerParams(dimension_semantics=("arbitrary",) * 3,
                                             vmem_limit_bytes=VMEM_LIMIT),
        name=name,
    )(q, k, v, *extra_inputs)


def _fox_attention(q, k, v, c):
    b, s, width = q.shape
    n_grp = width // LANES
    tq = TQ_ATTN
    crow = c[:, :, :FOX_HEADS].transpose(0, 2, 1).reshape(b, n_grp, HEADS_PER_GROUP, s)
    return _attention_call(
        _fox_kernel, "forgetting_attention", q, k, v, (c, crow),
        [pl.BlockSpec((1, s, LANES), lambda bi, g, i: (bi, 0, 0)),
         pl.BlockSpec((1, 1, HEADS_PER_GROUP, tq), lambda bi, g, i: (bi, g, 0, i))])


def _moba_kernel(q_ref, k_ref, v_ref, kmean_ref, o_ref, kaug_ref, vt_ref, s_ref):
    i = pl.program_id(2)
    tq = q_ref.shape[1]
    s_len = k_ref.shape[1]

    @pl.when(i == 0)
    def _():
        _stage_values(v_ref, vt_ref)
        rows = min(512, s_len)

        def chunk(ci, _):
            r0 = pl.multiple_of(ci * rows, rows)
            blk = (r0 + lax.broadcasted_iota(jnp.int32, (rows, LANES), 0)) // MOBA_BLOCK
            lane = lax.broadcasted_iota(jnp.int32, (rows, LANES), 1)
            _stage_keys(k_ref, kaug_ref, r0, rows,
                        jnp.where(lane - HEAD_DIM == blk, 1.0, 0.0), jnp.where(lane == blk, 1.0, 0.0))
            return 0

        lax.fori_loop(0, s_len // rows, chunk, 0)

    qt = q_ref[0].astype(F32).T
    kmean = kmean_ref[0]
    km_hi = kmean.astype(BF16)
    km_lo = (kmean - km_hi.astype(F32)).astype(BF16)
    zeros = jnp.zeros((HEAD_DIM, tq), F32)
    row = lax.broadcasted_iota(jnp.int32, (AUG_ROWS, tq), 0)
    row_f = row.astype(F32)
    own = (i * tq + lax.broadcasted_iota(jnp.int32, (AUG_ROWS, tq), 1)) // MOBA_BLOCK
    past = row < own
    exts = []
    for hh in range(HEADS_PER_GROUP):
        qz = (jnp.concatenate([qt[:HEAD_DIM], zeros], axis=0) if hh == 0
              else jnp.concatenate([zeros, qt[HEAD_DIM:]], axis=0)).astype(BF16)
        gate = (jnp.dot(km_hi, qz, preferred_element_type=F32)
                + jnp.dot(km_lo, qz, preferred_element_type=F32))
        gate = jnp.where(past, gate, NEG_INF)
        keep = row == own
        for _ in range(MOBA_TOPK):
            top = jnp.max(gate, axis=0, keepdims=True)
            idx = jnp.min(jnp.where(gate == top, row_f, float(AUG_ROWS)), axis=0, keepdims=True)
            pick = row_f == idx
            keep = keep | (pick & past)
            gate = jnp.where(pick, -jnp.inf, gate)
        exts.append(jnp.where(keep, 0.0, NEG_INF))
    _attend_pair(_augment_queries(qt, *exts), kaug_ref, vt_ref, s_ref, i, o_ref)


def _moba_attention(q, k, v, kmean):
    b, s, width = q.shape
    n_blk = s // MOBA_BLOCK
    assert n_blk <= AUG_ROWS and TK_ATTN == MOBA_BLOCK
    kmean = jnp.pad(kmean, ((0, 0), (0, AUG_ROWS - n_blk), (0, 0)))
    return _attention_call(
        _moba_kernel, "moba_attention", q, k, v, (kmean,),
        [pl.BlockSpec((1, AUG_ROWS, LANES), lambda bi, g, i: (bi, 0, g))])


def _merge_kernel(x_ref, gn_ref, oa_ref, ob_ref, oc_ref, om_ref, wg_ref, wbr_ref, wo_ref, o_ref):
    x = x_ref[...]
    d = x.shape[1]
    h = _rms(x, gn_ref[...]).astype(BF16)
    mix = jnp.zeros_like(x)
    for n, br in enumerate((oa_ref, ob_ref, oc_ref, om_ref)):
        gate = jax.nn.sigmoid(jnp.dot(h, wg_ref[:, n * d:(n + 1) * d], preferred_element_type=F32))
        y = jnp.dot(br[...], wbr_ref[n], preferred_element_type=F32)
        mix = mix + gate * y
    o_ref[...] = x + jnp.dot(mix.astype(BF16), wo_ref[...], preferred_element_type=F32)


def _merge(x2, norm_gain, branches, w_gate, w_branch, w_out):
    t, d = x2.shape
    bw = w_branch.shape[1]
    tm = min(TM_FFN, t)
    row = lambda w: pl.BlockSpec((tm, w), lambda i: (i, 0))
    return pl.pallas_call(
        _merge_kernel,
        out_shape=jax.ShapeDtypeStruct((t, d), F32),
        grid=(t // tm,),
        in_specs=[row(d), _resident((1, d)), row(bw), row(bw), row(bw), row(bw),
                  _resident((d, N_BRANCH * d)), _resident((N_BRANCH, bw, d)), _resident((d, d))],
        out_specs=row(d),
        compiler_params=_params(),
        name="gated_merge",
    )(x2, norm_gain[None, :], *branches, w_gate.astype(BF16), w_branch.astype(BF16), w_out.astype(BF16))


def kernel(x, mem, positions, norm_ffn1, w_ffn1_in, w_ffn1_out, norm_mix, w_in, b_forget, moba_q_gain, moba_k_gain, fox_q_gain, fox_k_gain, norm_mem, w_mem_kv, mem_q_gain, mem_k_gain, pool_mix, pool_scale, w_branch, w_out, norm_ffn2, w_ffn2_in, w_ffn2_out):
    b, s, d = x.shape
    depth = norm_ffn1.shape[0]
    gate_off = w_in.shape[2] - N_BRANCH * d
    cos, sin = _rope_tables(positions)
    x2 = x.reshape(b * s, d)
    for l in range(depth):
        x2 = _ffn(x2, norm_ffn1[l], w_ffn1_in[l], w_ffn1_out[l])
        k_m, v_m = _mem_kv(mem, norm_mem[l], w_mem_kv[l], mem_k_gain[l])
        qa, ka, va, qc, kc, vc, c, ob, om, kmean = _project(
            x2.reshape(b, s, d), norm_mix[l], w_in[l, :, :gate_off], b_forget[l],
            moba_q_gain[l], moba_k_gain[l], fox_q_gain[l], fox_k_gain[l],
            cos, sin, pool_mix[l], pool_scale[l], k_m, v_m, mem_q_gain[l])
        oa = _moba_attention(qa, ka, va, kmean)
        oc = _fox_attention(qc, kc, vc, c)
        flat = lambda a: a.reshape(b * s, a.shape[2])
        x2 = _merge(x2, norm_mix[l], (flat(oa), flat(ob), flat(oc), flat(om)),
                    w_in[l, :, gate_off:], w_branch[l], w_out[l])
        x2 = _ffn(x2, norm_ffn2[l], w_ffn2_in[l], w_ffn2_out[l])
    return x2.reshape(b, s, d)
```

```python
import jax
import jax.numpy as jnp
from jax import lax
from jax.experimental import pallas as pl
from jax.experimental.pallas import tpu as pltpu

HEAD_DIM = 64
MOBA_HEADS = 8
MOBA_BLOCK = 256
MOBA_TOPK = 3
FOX_HEADS = 8
POOL_WINDOWS = (2, 4, 8, 16)
POOL_GROUP = 128
MEM_HEADS = 4
MEM_HEAD_DIM = 128
ROPE_THETA = 500000.0
ROT_DIM = HEAD_DIM // 4
N_BRANCH = 4
RMS_EPS = 1e-6
NEG_INF = -1e30

LANES = 128
HEADS_PER_GROUP = LANES // HEAD_DIM
POOL_HALO = 16
VMEM_LIMIT = 56 * 1024 * 1024

TM_FFN = 512
TM_PROJ = 512
TQ_ATTN = 512
TK_ATTN = 256
AUG_ROWS = 16
FFN_CHUNK = 256

LOG2E = 1.4426950408889634
V_ROWS = HEAD_DIM + AUG_ROWS

F32 = jnp.float32
BF16 = jnp.bfloat16
_NT = (((1,), (1,)), ((), ()))


def _params():
    return pltpu.CompilerParams(vmem_limit_bytes=VMEM_LIMIT)


def _resident(shape):
    nd = len(shape)
    return pl.BlockSpec(shape, lambda *_: (0,) * nd, pipeline_mode=pl.Buffered(1))


def _rms(x, gain):
    ms = jnp.mean(x * x, axis=-1, keepdims=True)
    return x * lax.rsqrt(ms + RMS_EPS) * gain


def _head_pair_rms(x, gain):
    lane = lax.broadcasted_iota(jnp.int32, x.shape, 1)
    lo = lane < HEAD_DIM
    sq = x * x
    s_lo = jnp.sum(jnp.where(lo, sq, 0.0), axis=-1, keepdims=True)
    s_hi = jnp.sum(jnp.where(lo, 0.0, sq), axis=-1, keepdims=True)
    ms = jnp.where(lo, s_lo, s_hi) * (1.0 / HEAD_DIM)
    return x * lax.rsqrt(ms + RMS_EPS) * gain


def _rotate(x, cos, sin):
    lane = lax.broadcasted_iota(jnp.int32, x.shape, 1)
    half = ROT_DIM // 2
    fwd = pltpu.roll(x, LANES - half, axis=1)
    bwd = pltpu.roll(x, half, axis=1)
    partner = jnp.where((lane % HEAD_DIM) < half, fwd, bwd)
    return x * cos + partner * sin


def _rope_kernel(pos_ref, freq_ref, sign_ref, cos_ref, sin_ref):
    ang = freq_ref[...] * pos_ref[0].astype(F32)
    cos_r = jnp.cos(ang)
    sin_r = jnp.sin(ang) * sign_ref[...]
    ts = ang.shape[1]
    one = jnp.ones((HEAD_DIM - ROT_DIM, ts), F32)
    zero = jnp.zeros((HEAD_DIM - ROT_DIM, ts), F32)
    cos_ref[0] = jnp.concatenate([cos_r, one] * HEADS_PER_GROUP, axis=0).T
    sin_ref[0] = jnp.concatenate([sin_r, zero] * HEADS_PER_GROUP, axis=0).T


def _rope_tables(positions):
    b, s = positions.shape
    half = ROT_DIM // 2
    inv_freq = jnp.power(ROPE_THETA, -jnp.arange(half, dtype=F32) * (2.0 / ROT_DIM))
    freq = jnp.tile(inv_freq, 2)[:, None]
    sign = jnp.concatenate([-jnp.ones((half,), F32), jnp.ones((half,), F32)])[:, None]
    ts = min(s, 1024)
    return pl.pallas_call(
        _rope_kernel,
        out_shape=(jax.ShapeDtypeStruct((b, s, LANES), F32),) * 2,
        grid=(b, s // ts),
        in_specs=[pl.BlockSpec((1, 1, ts), lambda i, j: (i, 0, j)),
                  pl.BlockSpec((ROT_DIM, 1), lambda i, j: (0, 0)),
                  pl.BlockSpec((ROT_DIM, 1), lambda i, j: (0, 0))],
        out_specs=(pl.BlockSpec((1, ts, LANES), lambda i, j: (i, j, 0)),) * 2,
        compiler_params=_params(),
        name="rope_tables",
    )(positions.reshape(b, 1, s), freq, sign)


def _ffn_kernel(x_ref, g_ref, wg_ref, wu_ref, wo_ref, o_ref):
    x = x_ref[...]
    h = _rms(x, g_ref[...]).astype(BF16)
    d_ff = wg_ref.shape[1]
    acc = jnp.zeros_like(x)
    for c in range(d_ff // FFN_CHUNK):
        sl = slice(c * FFN_CHUNK, (c + 1) * FFN_CHUNK)
        g = jnp.dot(h, wg_ref[:, sl], preferred_element_type=F32)
        u = jnp.dot(h, wu_ref[:, sl], preferred_element_type=F32)
        a = (g * jax.nn.sigmoid(g) * u).astype(BF16)
        acc = acc + jnp.dot(a, wo_ref[sl, :], preferred_element_type=F32)
    o_ref[...] = x + 0.5 * acc


def _ffn(x2, gain, w_in, w_out):
    t, d = x2.shape
    d_ff = w_out.shape[0]
    wg = w_in[:, :d_ff].astype(BF16)
    wu = w_in[:, d_ff:].astype(BF16)
    wo = w_out.astype(BF16)
    tm = min(TM_FFN, t)
    return pl.pallas_call(
        _ffn_kernel,
        out_shape=jax.ShapeDtypeStruct((t, d), F32),
        grid=(t // tm,),
        in_specs=[pl.BlockSpec((tm, d), lambda i: (i, 0)),
                  _resident((1, d)), _resident((d, d_ff)), _resident((d, d_ff)), _resident((d_ff, d))],
        out_specs=pl.BlockSpec((tm, d), lambda i: (i, 0)),
        compiler_params=_params(),
        name="ffn_half_step",
    )(x2, gain[None, :], wg, wu, wo)


def _memkv_kernel(mem_ref, gn_ref, w_ref, kg_ref, k_ref, v_ref):
    width = k_ref.shape[2]
    mn = _rms(mem_ref[0], gn_ref[...]).astype(BF16)
    kv = jnp.dot(mn, w_ref[...], preferred_element_type=F32)
    for h in range(MEM_HEADS):
        sl = slice(h * MEM_HEAD_DIM, (h + 1) * MEM_HEAD_DIM)
        k_ref[0, :, sl] = _rms(kv[:, sl], kg_ref[...]).astype(BF16)
    v_ref[0] = kv[:, width:].astype(BF16)


def _mem_kv(mem, norm_gain, w_kv, k_gain):
    b, m, d = mem.shape
    width = MEM_HEADS * MEM_HEAD_DIM
    return pl.pallas_call(
        _memkv_kernel,
        out_shape=(jax.ShapeDtypeStruct((b, m, width), BF16),) * 2,
        grid=(b,),
        in_specs=[pl.BlockSpec((1, m, d), lambda i: (i, 0, 0)),
                  _resident((1, d)), _resident((d, 2 * width)), _resident((1, MEM_HEAD_DIM))],
        out_specs=(pl.BlockSpec((1, m, width), lambda i: (i, 0, 0)),) * 2,
        compiler_params=_params(),
        name="memory_kv",
    )(mem, norm_gain[None, :], w_kv.astype(BF16), k_gain[None, :])


def _proj_kernel(x_ref, gn_ref, wqkv_ref, wf_ref, wp_ref, wmq_ref, bf_ref, gains_ref,
                 cos_ref, sin_ref, pmix_ref, pscale_ref, km_ref, vm_ref, mqg_ref,
                 qa_ref, ka_ref, va_ref, qc_ref, kc_ref, vc_ref, c_ref, ob_ref, om_ref, kmean_ref,
                 pool_carry, c_carry):
    j = pl.program_id(1)
    tm = x_ref.shape[1]
    width = qa_ref.shape[2]
    n_grp = width // LANES

    @pl.when(j == 0)
    def _():
        pool_carry[...] = jnp.zeros_like(pool_carry)
        c_carry[...] = jnp.zeros_like(c_carry)

    h = _rms(x_ref[0], gn_ref[...]).astype(BF16)
    scale = HEAD_DIM ** -0.5 * LOG2E

    cos = cos_ref[0]
    sin = sin_ref[0]
    kmean_ref[...] = jnp.zeros_like(kmean_ref)
    qkv = jnp.dot(h, wqkv_ref[...], preferred_element_type=F32)
    for idx, (q_out, k_out, v_out, rot) in enumerate(((qa_ref, ka_ref, va_ref, True),
                                                      (qc_ref, kc_ref, vc_ref, False))):
        base = idx * 3 * width
        for g in range(n_grp):
            sl = slice(g * LANES, (g + 1) * LANES)
            q = qkv[:, base + g * LANES: base + (g + 1) * LANES]
            k = qkv[:, base + width + g * LANES: base + width + (g + 1) * LANES]
            v = qkv[:, base + 2 * width + g * LANES: base + 2 * width + (g + 1) * LANES]
            q = _head_pair_rms(q, gains_ref[2 * idx: 2 * idx + 1, :])
            k = _head_pair_rms(k, gains_ref[2 * idx + 1: 2 * idx + 2, :])
            if rot:
                q = _rotate(q, cos, sin)
                k = _rotate(k, cos, sin)
                for sb in range(tm // MOBA_BLOCK):
                    kmean_ref[0, 0, sb:sb + 1, sl] = jnp.mean(
                        k[sb * MOBA_BLOCK:(sb + 1) * MOBA_BLOCK], axis=0, keepdims=True)
            q_out[0, :, sl] = (q * scale).astype(BF16)
            k_out[0, :, sl] = k.astype(BF16)
            v_out[0, :, sl] = v.astype(BF16)

    lf = jax.nn.log_sigmoid(jnp.dot(h, wf_ref[...], preferred_element_type=F32) + bf_ref[...])
    row = lax.broadcasted_iota(jnp.int32, lf.shape, 0)
    step = 1
    while step < tm:
        lf = lf + jnp.where(row >= step, pltpu.roll(lf, step, axis=0), 0.0)
        step *= 2
    c = lf + c_carry[...]
    c_ref[0] = c
    c_carry[...] = c[tm - 1:tm, :]

    u = jnp.dot(h, wp_ref[...], preferred_element_type=F32)
    ext = jnp.concatenate([pool_carry[...], u], axis=0)
    pos = j * tm + lax.broadcasted_iota(jnp.int32, (tm, 1), 0)
    for g, w in enumerate(POOL_WINDOWS):
        sl = slice(g * POOL_GROUP, (g + 1) * POOL_GROUP)
        win = ext[:, sl]
        step = 1
        while step < w:
            win = win + pltpu.roll(win, step, axis=0)
            step *= 2
        cnt = jnp.minimum(pos + 1, w).astype(F32)
        pooled = (win[POOL_HALO:, :] / cnt - u[:, sl]).astype(BF16)
        mixed = jnp.dot(pooled, pmix_ref[g], preferred_element_type=F32)
        ob_ref[0, :, sl] = (mixed * pscale_ref[:, sl]).astype(BF16)
    pool_carry[...] = u[tm - POOL_HALO:, :]

    qm_all = jnp.dot(h, wmq_ref[...], preferred_element_type=F32)
    for hd in range(MEM_HEADS):
        sl = slice(hd * MEM_HEAD_DIM, (hd + 1) * MEM_HEAD_DIM)
        qm = _rms(qm_all[:, sl], mqg_ref[...]).astype(BF16)
        s = lax.dot_general(qm, km_ref[0, :, sl], _NT, preferred_element_type=F32) * (MEM_HEAD_DIM ** -0.5)
        p = jnp.exp(s - jnp.max(s, axis=-1, keepdims=True))
        den = jnp.sum(p, axis=-1, keepdims=True)
        o = jnp.dot(p.astype(BF16), vm_ref[0, :, sl], preferred_element_type=F32)
        om_ref[0, :, sl] = (o / den).astype(BF16)


def _project(x3, norm_gain, w_in, b_forget, moba_q_gain, moba_k_gain, fox_q_gain, fox_k_gain,
             cos, sin, pool_mix, pool_scale, k_m, v_m, mem_q_gain):
    b, s, d = x3.shape
    width = MOBA_HEADS * HEAD_DIM
    pool_width = POOL_GROUP * len(POOL_WINDOWS)
    mem_width = MEM_HEADS * MEM_HEAD_DIM
    m_len = k_m.shape[1]
    tm = min(TM_PROJ, s)
    nt = s // tm
    o0 = 6 * width
    w_qkv = w_in[:, :o0].astype(BF16)
    w_f = jnp.pad(w_in[:, o0:o0 + FOX_HEADS], ((0, 0), (0, LANES - FOX_HEADS))).astype(BF16)
    b_f = jnp.pad(b_forget, (0, LANES - FOX_HEADS))[None, :]
    o1 = o0 + FOX_HEADS
    w_pool = w_in[:, o1:o1 + pool_width].astype(BF16)
    o2 = o1 + pool_width
    w_mq = w_in[:, o2:o2 + mem_width].astype(BF16)
    gains = jnp.stack([jnp.tile(g, HEADS_PER_GROUP)
                       for g in (moba_q_gain, moba_k_gain, fox_q_gain, fox_k_gain)])

    tile = lambda w: pl.BlockSpec((1, tm, w), lambda i, j: (i, j, 0))
    act = lambda w: jax.ShapeDtypeStruct((b, s, w), BF16)
    outs = pl.pallas_call(
        _proj_kernel,
        out_shape=(act(width),) * 6 + (jax.ShapeDtypeStruct((b, s, LANES), F32), act(pool_width),
                                       act(mem_width), jax.ShapeDtypeStruct((b, nt, 8, width), F32)),
        grid=(b, nt),
        in_specs=[tile(d), _resident((1, d)), _resident((d, o0)), _resident((d, LANES)),
                  _resident((d, pool_width)), _resident((d, mem_width)), _resident((1, LANES)),
                  _resident((4, LANES)), tile(LANES), tile(LANES),
                  _resident((len(POOL_WINDOWS), POOL_GROUP, POOL_GROUP)), _resident((1, pool_width)),
                  pl.BlockSpec((1, m_len, mem_width), lambda i, j: (i, 0, 0)),
                  pl.BlockSpec((1, m_len, mem_width), lambda i, j: (i, 0, 0)),
                  _resident((1, MEM_HEAD_DIM))],
        out_specs=(tile(width),) * 6 + (tile(LANES), tile(pool_width), tile(mem_width),
                                        pl.BlockSpec((1, 1, 8, width), lambda i, j: (i, j, 0, 0))),
        scratch_shapes=[pltpu.VMEM((POOL_HALO, pool_width), F32), pltpu.VMEM((1, LANES), F32)],
        compiler_params=pltpu.CompilerParams(dimension_semantics=("arbitrary", "arbitrary"),
                                             vmem_limit_bytes=VMEM_LIMIT),
        name="mixer_projection",
    )(x3, norm_gain[None, :], w_qkv, w_f, w_pool, w_mq, b_f, gains, cos, sin,
      pool_mix.astype(BF16), pool_scale[None, :], k_m, v_m, mem_q_gain[None, :])
    qa, ka, va, qc, kc, vc, c, ob, om, kmean = outs
    kmean = kmean[:, :, :tm // MOBA_BLOCK, :].reshape(b, s // MOBA_BLOCK, width)
    return qa, ka, va, qc, kc, vc, c, ob, om, kmean


def _stage_values(v_ref, vt_ref):
    vt = v_ref[0].astype(F32).T
    s_len = vt.shape[1]
    row = lax.broadcasted_iota(jnp.int32, (AUG_ROWS, s_len), 0)
    tail = jnp.where(row == 0, 1.0, 0.0)
    for hh in range(HEADS_PER_GROUP):
        vt_ref[hh] = jnp.concatenate([vt[hh * HEAD_DIM:(hh + 1) * HEAD_DIM], tail], axis=0).astype(BF16)


def _stage_keys(k_ref, kaug_ref, r0, rows, ext0, ext1):
    k = k_ref[0, pl.ds(r0, rows), :]
    lane = lax.broadcasted_iota(jnp.int32, k.shape, 1)
    kaug_ref[0, pl.ds(r0, rows), :] = jnp.where(lane < HEAD_DIM, k, ext0.astype(BF16))
    kaug_ref[1, pl.ds(r0, rows), :] = jnp.where(lane >= HEAD_DIM, k, ext1.astype(BF16))


def _augment_queries(qt, e0, e1):
    tq = qt.shape[1]
    pad = jnp.zeros((HEAD_DIM - AUG_ROWS, tq), F32)
    qa0 = jnp.concatenate([qt[:HEAD_DIM], e0, pad], axis=0).astype(BF16)
    qa1 = jnp.concatenate([e1, pad, qt[HEAD_DIM:]], axis=0).astype(BF16)
    return qa0, qa1


def _attend_pair(qa, kaug_ref, vt_ref, s_ref, i, o_ref):
    tq = o_ref.shape[0]
    tk = TK_ATTN
    key_i = lax.broadcasted_iota(jnp.int32, (tk, tq), 0)
    qry_i = lax.broadcasted_iota(jnp.int32, (tk, tq), 1)

    def issue_scores(jb, slot):
        k0 = pl.multiple_of(jb * tk, tk)
        for hh in range(HEADS_PER_GROUP):
            s_ref[slot, hh] = jnp.dot(kaug_ref[hh, pl.ds(k0, tk), :], qa[hh],
                                      preferred_element_type=F32)

    def block_max(slot, diag=None):
        out = []
        for hh in range(HEADS_PER_GROUP):
            s = s_ref[slot, hh]
            if diag is not None:
                s = jnp.where(key_i + diag * tk <= qry_i, s, NEG_INF)
            out.append(jnp.max(s, axis=0, keepdims=True))
        return tuple(out)

    def accumulate(jb, slot, carry, mx, diag=None):
        k0 = pl.multiple_of(jb * tk, tk)
        new = []
        for hh in range(HEADS_PER_GROUP):
            m, acc = carry[hh]
            s = s_ref[slot, hh]
            if diag is not None:
                s = jnp.where(key_i + diag * tk <= qry_i, s, NEG_INF)
            m_new = jnp.maximum(m, mx[hh])
            alpha = jnp.exp2(m - m_new)
            p = jnp.exp2(s - m_new).astype(BF16)
            vb = vt_ref[hh, :, pl.ds(k0, tk)]
            new.append((m_new, alpha * acc + jnp.dot(vb, p, preferred_element_type=F32)))
        return tuple(new)

    def step(t, state):
        carry, mx = state
        issue_scores(2 * t + 1, 1)
        carry = accumulate(2 * t, 0, carry, mx)
        mx = block_max(1)
        issue_scores(2 * t + 2, 0)
        carry = accumulate(2 * t + 1, 1, carry, mx)
        return carry, block_max(0)

    init = tuple((jnp.full((1, tq), NEG_INF, F32), jnp.zeros((V_ROWS, tq), F32))
                 for _ in range(HEADS_PER_GROUP))
    issue_scores(0, 0)
    state = lax.fori_loop(0, i // 2, lambda u, st: step(2 * u + 1, step(2 * u, st)), (init, block_max(0)))
    carry, _ = lax.cond(i % 2 == 1, lambda st: step(i - 1, st), lambda st: st, state)
    issue_scores(2 * i + 1, 1)
    carry = accumulate(2 * i, 0, carry, block_max(0, diag=0), diag=0)
    carry = accumulate(2 * i + 1, 1, carry, block_max(1, diag=1), diag=1)
    ot = jnp.concatenate([acc[:HEAD_DIM] / acc[HEAD_DIM:HEAD_DIM + 1] for (_, acc) in carry], axis=0)
    o_ref[...] = ot.T.astype(o_ref.dtype)


def _split3(x):
    hi = x.astype(BF16).astype(F32)
    r = x - hi
    mid = r.astype(BF16).astype(F32)
    return hi, mid, r - mid


def _fox_kernel(q_ref, k_ref, v_ref, c_ref, crow_ref, o_ref, kaug_ref, vt_ref, s_ref):
    g = pl.program_id(1)
    tq = TQ_ATTN
    s_len = k_ref.shape[1]
    n_parts = 3

    _stage_values(v_ref, vt_ref)
    r = lax.broadcasted_iota(jnp.int32, (LANES, 2 * LANES), 0)
    col = lax.broadcasted_iota(jnp.int32, (LANES, 2 * LANES), 1)
    place = []
    for part in range(n_parts):
        hit = (((r == HEADS_PER_GROUP * g) & (col == HEAD_DIM + n_parts + part))
               | ((r == HEADS_PER_GROUP * g + 1) & (col == LANES + n_parts + part)))
        place.append(jnp.where(hit, -1.0, 0.0).astype(BF16))
    lane2 = lax.broadcasted_iota(jnp.int32, (1, 2 * LANES), 1)
    ones = jnp.where(((lane2 >= HEAD_DIM) & (lane2 < HEAD_DIM + n_parts))
                     | ((lane2 >= LANES) & (lane2 < LANES + n_parts)), 1.0, 0.0)
    rows = min(512, s_len)

    def chunk(ci, _):
        r0 = pl.multiple_of(ci * rows, rows)
        parts = _split3(c_ref[0, pl.ds(r0, rows), :] * LOG2E)
        ext = ones + sum(jnp.dot(p.astype(BF16), w, preferred_element_type=F32)
                         for p, w in zip(parts, place))
        _stage_keys(k_ref, kaug_ref, r0, rows, ext[:, :LANES], ext[:, LANES:])
        return 0

    lax.fori_loop(0, s_len // rows, chunk, 0)

    row = lax.broadcasted_iota(jnp.int32, (AUG_ROWS, tq), 0)

    def tile(i, _):
        q0 = pl.multiple_of(i * tq, tq)
        qt = q_ref[0, pl.ds(q0, tq), :].astype(F32).T
        exts = []
        for hh in range(HEADS_PER_GROUP):
            hi, mid, lo = _split3(crow_ref[0, 0, hh:hh + 1, pl.ds(q0, tq)] * LOG2E)
            e = jnp.where(row == 0, hi, jnp.where(row == 1, mid, jnp.where(row == 2, lo, 0.0)))
            exts.append(jnp.where((row >= n_parts) & (row < 2 * n_parts), 1.0, e))
        _attend_pair(_augment_queries(qt, *exts), kaug_ref, vt_ref, s_ref, i, o_ref.at[0, pl.ds(q0, tq), :])
        return 0

    lax.fori_loop(0, s_len // tq, tile, 0)


def _attention_call(kernel_fn, name, q, k, v, extra_inputs, extra_specs):
    b, s, width = q.shape
    n_grp = width // LANES
    tq = TQ_ATTN
    assert tq == 2 * TK_ATTN and s % tq == 0
    seq = pl.BlockSpec((1, s, LANES), lambda bi, g: (bi, 0, g))
    return pl.pallas_call(
        kernel_fn,
        out_shape=jax.ShapeDtypeStruct((b, s, width), BF16),
        grid=(b, n_grp),
        in_specs=[seq, seq, seq] + extra_specs,
        out_specs=seq,
        scratch_shapes=[pltpu.VMEM((HEADS_PER_GROUP, s, LANES), BF16),
                        pltpu.VMEM((HEADS_PER_GROUP, V_ROWS, s), BF16),
                        pltpu.VMEM((2, HEADS_PER_GROUP, TK_ATTN, tq), F32)],
        compiler_params=_params(),
        name=name,
    )(q, k, v, *extra_inputs)


def _fox_attention(q, k, v, c):
    b, s, width = q.shape
    n_grp = width // LANES
    crow = c[:, :, :FOX_HEADS].transpose(0, 2, 1).reshape(b, n_grp, HEADS_PER_GROUP, s)
    return _attention_call(
        _fox_kernel, "forgetting_attention", q, k, v, (c, crow),
        [pl.BlockSpec((1, s, LANES), lambda bi, g: (bi, 0, 0)),
         pl.BlockSpec((1, 1, HEADS_PER_GROUP, s), lambda bi, g: (bi, g, 0, 0))])


def _moba_kernel(q_ref, k_ref, v_ref, kmean_ref, o_ref, kaug_ref, vt_ref, s_ref):
    tq = TQ_ATTN
    s_len = k_ref.shape[1]

    _stage_values(v_ref, vt_ref)
    rows = min(512, s_len)

    def chunk(ci, _):
        r0 = pl.multiple_of(ci * rows, rows)
        blk = (r0 + lax.broadcasted_iota(jnp.int32, (rows, LANES), 0)) // MOBA_BLOCK
        lane = lax.broadcasted_iota(jnp.int32, (rows, LANES), 1)
        _stage_keys(k_ref, kaug_ref, r0, rows,
                    jnp.where(lane - HEAD_DIM == blk, 1.0, 0.0), jnp.where(lane == blk, 1.0, 0.0))
        return 0

    lax.fori_loop(0, s_len // rows, chunk, 0)

    kmean = kmean_ref[0]
    km_hi = kmean.astype(BF16)
    km_lo = (kmean - km_hi.astype(F32)).astype(BF16)
    zeros = jnp.zeros((HEAD_DIM, tq), F32)
    row = lax.broadcasted_iota(jnp.int32, (AUG_ROWS, tq), 0)
    row_f = row.astype(F32)
    qpos = lax.broadcasted_iota(jnp.int32, (AUG_ROWS, tq), 1)

    def tile(i, _):
        q0 = pl.multiple_of(i * tq, tq)
        qt = q_ref[0, pl.ds(q0, tq), :].astype(F32).T
        own = (q0 + qpos) // MOBA_BLOCK
        past = row < own
        exts = []
        for hh in range(HEADS_PER_GROUP):
            qz = (jnp.concatenate([qt[:HEAD_DIM], zeros], axis=0) if hh == 0
                  else jnp.concatenate([zeros, qt[HEAD_DIM:]], axis=0)).astype(BF16)
            gate = (jnp.dot(km_hi, qz, preferred_element_type=F32)
                    + jnp.dot(km_lo, qz, preferred_element_type=F32))
            gate = jnp.where(past, gate, NEG_INF)
            keep = row == own
            for _ in range(MOBA_TOPK):
                top = jnp.max(gate, axis=0, keepdims=True)
                idx = jnp.min(jnp.where(gate == top, row_f, float(AUG_ROWS)), axis=0, keepdims=True)
                pick = row_f == idx
                keep = keep | (pick & past)
                gate = jnp.where(pick, -jnp.inf, gate)
            exts.append(jnp.where(keep, 0.0, NEG_INF))
        _attend_pair(_augment_queries(qt, *exts), kaug_ref, vt_ref, s_ref, i, o_ref.at[0, pl.ds(q0, tq), :])
        return 0

    lax.fori_loop(0, s_len // tq, tile, 0)


def _moba_attention(q, k, v, kmean):
    b, s, width = q.shape
    n_blk = s // MOBA_BLOCK
    assert n_blk <= AUG_ROWS and TK_ATTN == MOBA_BLOCK
    kmean = jnp.pad(kmean, ((0, 0), (0, AUG_ROWS - n_blk), (0, 0)))
    return _attention_call(
        _moba_kernel, "moba_attention", q, k, v, (kmean,),
        [pl.BlockSpec((1, AUG_ROWS, LANES), lambda bi, g: (bi, 0, g))])


def _merge_kernel(x_ref, gn_ref, oa_ref, ob_ref, oc_ref, om_ref, wg_ref, wbr_ref, wo_ref, o_ref):
    x = x_ref[...]
    d = x.shape[1]
    h = _rms(x, gn_ref[...]).astype(BF16)
    mix = jnp.zeros_like(x)
    for n, br in enumerate((oa_ref, ob_ref, oc_ref, om_ref)):
        gate = jax.nn.sigmoid(jnp.dot(h, wg_ref[:, n * d:(n + 1) * d], preferred_element_type=F32))
        y = jnp.dot(br[...], wbr_ref[n], preferred_element_type=F32)
        mix = mix + gate * y
    o_ref[...] = x + jnp.dot(mix.astype(BF16), wo_ref[...], preferred_element_type=F32)


def _merge(x2, norm_gain, branches, w_gate, w_branch, w_out):
    t, d = x2.shape
    bw = w_branch.shape[1]
    tm = min(TM_FFN, t)
    row = lambda w: pl.BlockSpec((tm, w), lambda i: (i, 0))
    return pl.pallas_call(
        _merge_kernel,
        out_shape=jax.ShapeDtypeStruct((t, d), F32),
        grid=(t // tm,),
        in_specs=[row(d), _resident((1, d)), row(bw), row(bw), row(bw), row(bw),
                  _resident((d, N_BRANCH * d)), _resident((N_BRANCH, bw, d)), _resident((d, d))],
        out_specs=row(d),
        compiler_params=_params(),
        name="gated_merge",
    )(x2, norm_gain[None, :], *branches, w_gate.astype(BF16), w_branch.astype(BF16), w_out.astype(BF16))


def kernel(x, mem, positions, norm_ffn1, w_ffn1_in, w_ffn1_out, norm_mix, w_in, b_forget, moba_q_gain, moba_k_gain, fox_q_gain, fox_k_gain, norm_mem, w_mem_kv, mem_q_gain, mem_k_gain, pool_mix, pool_scale, w_branch, w_out, norm_ffn2, w_ffn2_in, w_ffn2_out):
    b, s, d = x.shape
    depth = norm_ffn1.shape[0]
    gate_off = w_in.shape[2] - N_BRANCH * d
    cos, sin = _rope_tables(positions)
    x2 = x.reshape(b * s, d)
    for l in range(depth):
        x2 = _ffn(x2, norm_ffn1[l], w_ffn1_in[l], w_ffn1_out[l])
        k_m, v_m = _mem_kv(mem, norm_mem[l], w_mem_kv[l], mem_k_gain[l])
        qa, ka, va, qc, kc, vc, c, ob, om, kmean = _project(
            x2.reshape(b, s, d), norm_mix[l], w_in[l, :, :gate_off], b_forget[l],
            moba_q_gain[l], moba_k_gain[l], fox_q_gain[l], fox_k_gain[l],
            cos, sin, pool_mix[l], pool_scale[l], k_m, v_m, mem_q_gain[l])
        oa = _moba_attention(qa, ka, va, kmean)
        oc = _fox_attention(qc, kc, vc, c)
        flat = lambda a: a.reshape(b * s, a.shape[2])
        x2 = _merge(x2, norm_mix[l], (flat(oa), flat(ob), flat(oc), flat(om)),
                    w_in[l, :, gate_off:], w_branch[l], w_out[l])
        x2 = _ffn(x2, norm_ffn2[l], w_ffn2_in[l], w_ffn2_out[l])
    return x2.reshape(b, s, d)
```

```python
import jax
import jax.numpy as jnp
from jax import lax
from jax.experimental import pallas as pl
from jax.experimental.pallas import tpu as pltpu

HEAD_DIM = 64
MOBA_HEADS = 8
MOBA_BLOCK = 256
MOBA_TOPK = 3
FOX_HEADS = 8
POOL_WINDOWS = (2, 4, 8, 16)
POOL_GROUP = 128
MEM_HEADS = 4
MEM_HEAD_DIM = 128
ROPE_THETA = 500000.0
ROT_DIM = HEAD_DIM // 4
N_BRANCH = 4
RMS_EPS = 1e-6
NEG_INF = -1e30

LANES = 128
HEADS_PER_GROUP = LANES // HEAD_DIM
POOL_HALO = 16
VMEM_LIMIT = 56 * 1024 * 1024

TM_FFN = 512
TM_PROJ = 512
TQ_ATTN = 512
TK_ATTN = 256
AUG_ROWS = 16
FFN_CHUNK = 256

LOG2E = 1.4426950408889634
V_ROWS = HEAD_DIM + AUG_ROWS

F32 = jnp.float32
BF16 = jnp.bfloat16
_NT = (((1,), (1,)), ((), ()))


def _params():
    return pltpu.CompilerParams(vmem_limit_bytes=VMEM_LIMIT)


def _resident(shape):
    nd = len(shape)
    return pl.BlockSpec(shape, lambda *_: (0,) * nd, pipeline_mode=pl.Buffered(1))


def _rms(x, gain):
    ms = jnp.mean(x * x, axis=-1, keepdims=True)
    return x * lax.rsqrt(ms + RMS_EPS) * gain


def _head_pair_rms(x, gain):
    lane = lax.broadcasted_iota(jnp.int32, x.shape, 1)
    lo = lane < HEAD_DIM
    sq = x * x
    s_lo = jnp.sum(jnp.where(lo, sq, 0.0), axis=-1, keepdims=True)
    s_hi = jnp.sum(jnp.where(lo, 0.0, sq), axis=-1, keepdims=True)
    ms = jnp.where(lo, s_lo, s_hi) * (1.0 / HEAD_DIM)
    return x * lax.rsqrt(ms + RMS_EPS) * gain


def _rotate(x, cos, sin):
    lane = lax.broadcasted_iota(jnp.int32, x.shape, 1)
    half = ROT_DIM // 2
    fwd = pltpu.roll(x, LANES - half, axis=1)
    bwd = pltpu.roll(x, half, axis=1)
    partner = jnp.where((lane % HEAD_DIM) < half, fwd, bwd)
    return x * cos + partner * sin


def _rope_kernel(pos_ref, freq_ref, sign_ref, cos_ref, sin_ref):
    ang = freq_ref[...] * pos_ref[0].astype(F32)
    cos_r = jnp.cos(ang)
    sin_r = jnp.sin(ang) * sign_ref[...]
    ts = ang.shape[1]
    one = jnp.ones((HEAD_DIM - ROT_DIM, ts), F32)
    zero = jnp.zeros((HEAD_DIM - ROT_DIM, ts), F32)
    cos_ref[0] = jnp.concatenate([cos_r, one] * HEADS_PER_GROUP, axis=0).T
    sin_ref[0] = jnp.concatenate([sin_r, zero] * HEADS_PER_GROUP, axis=0).T


def _rope_tables(positions):
    b, s = positions.shape
    half = ROT_DIM // 2
    inv_freq = jnp.power(ROPE_THETA, -jnp.arange(half, dtype=F32) * (2.0 / ROT_DIM))
    freq = jnp.tile(inv_freq, 2)[:, None]
    sign = jnp.concatenate([-jnp.ones((half,), F32), jnp.ones((half,), F32)])[:, None]
    ts = min(s, 1024)
    return pl.pallas_call(
        _rope_kernel,
        out_shape=(jax.ShapeDtypeStruct((b, s, LANES), F32),) * 2,
        grid=(b, s // ts),
        in_specs=[pl.BlockSpec((1, 1, ts), lambda i, j: (i, 0, j)),
                  pl.BlockSpec((ROT_DIM, 1), lambda i, j: (0, 0)),
                  pl.BlockSpec((ROT_DIM, 1), lambda i, j: (0, 0))],
        out_specs=(pl.BlockSpec((1, ts, LANES), lambda i, j: (i, j, 0)),) * 2,
        compiler_params=_params(),
        name="rope_tables",
    )(positions.reshape(b, 1, s), freq, sign)


def _ffn_kernel(x_ref, g_ref, wg_ref, wu_ref, wo_ref, o_ref):
    x = x_ref[...]
    h = _rms(x, g_ref[...]).astype(BF16)
    d_ff = wg_ref.shape[1]
    acc = jnp.zeros_like(x)
    for c in range(d_ff // FFN_CHUNK):
        sl = slice(c * FFN_CHUNK, (c + 1) * FFN_CHUNK)
        g = jnp.dot(h, wg_ref[:, sl], preferred_element_type=F32)
        u = jnp.dot(h, wu_ref[:, sl], preferred_element_type=F32)
        a = (g * jax.nn.sigmoid(g) * u).astype(BF16)
        acc = acc + jnp.dot(a, wo_ref[sl, :], preferred_element_type=F32)
    o_ref[...] = x + 0.5 * acc


def _ffn(x2, gain, w_in, w_out):
    t, d = x2.shape
    d_ff = w_out.shape[0]
    wg = w_in[:, :d_ff].astype(BF16)
    wu = w_in[:, d_ff:].astype(BF16)
    wo = w_out.astype(BF16)
    tm = min(TM_FFN, t)
    return pl.pallas_call(
        _ffn_kernel,
        out_shape=jax.ShapeDtypeStruct((t, d), F32),
        grid=(t // tm,),
        in_specs=[pl.BlockSpec((tm, d), lambda i: (i, 0)),
                  _resident((1, d)), _resident((d, d_ff)), _resident((d, d_ff)), _resident((d_ff, d))],
        out_specs=pl.BlockSpec((tm, d), lambda i: (i, 0)),
        compiler_params=_params(),
        name="ffn_half_step",
    )(x2, gain[None, :], wg, wu, wo)


def _memkv_kernel(mem_ref, gn_ref, w_ref, kg_ref, k_ref, v_ref):
    width = k_ref.shape[2]
    mn = _rms(mem_ref[0], gn_ref[...]).astype(BF16)
    kv = jnp.dot(mn, w_ref[...], preferred_element_type=F32)
    for h in range(MEM_HEADS):
        sl = slice(h * MEM_HEAD_DIM, (h + 1) * MEM_HEAD_DIM)
        k_ref[0, :, sl] = _rms(kv[:, sl], kg_ref[...]).astype(BF16)
    v_ref[0] = kv[:, width:].astype(BF16)


def _mem_kv(mem, norm_gain, w_kv, k_gain):
    b, m, d = mem.shape
    width = MEM_HEADS * MEM_HEAD_DIM
    return pl.pallas_call(
        _memkv_kernel,
        out_shape=(jax.ShapeDtypeStruct((b, m, width), BF16),) * 2,
        grid=(b,),
        in_specs=[pl.BlockSpec((1, m, d), lambda i: (i, 0, 0)),
                  _resident((1, d)), _resident((d, 2 * width)), _resident((1, MEM_HEAD_DIM))],
        out_specs=(pl.BlockSpec((1, m, width), lambda i: (i, 0, 0)),) * 2,
        compiler_params=_params(),
        name="memory_kv",
    )(mem, norm_gain[None, :], w_kv.astype(BF16), k_gain[None, :])


def _proj_kernel(x_ref, gn_ref, wqkv_ref, wf_ref, wp_ref, wmq_ref, bf_ref, gains_ref,
                 cos_ref, sin_ref, pmix_ref, pscale_ref, km_ref, vm_ref, mqg_ref,
                 qa_ref, ka_ref, va_ref, qc_ref, kc_ref, vc_ref, c_ref, ob_ref, om_ref, kmean_ref,
                 pool_carry, c_carry):
    j = pl.program_id(1)
    tm = x_ref.shape[1]
    width = qa_ref.shape[2]
    n_grp = width // LANES

    @pl.when(j == 0)
    def _():
        pool_carry[...] = jnp.zeros_like(pool_carry)
        c_carry[...] = jnp.zeros_like(c_carry)

    h = _rms(x_ref[0], gn_ref[...]).astype(BF16)
    scale = HEAD_DIM ** -0.5 * LOG2E
    cos = cos_ref[0]
    sin = sin_ref[0]
    kmean_ref[...] = jnp.zeros_like(kmean_ref)

    def project(w_ref, lo=None, hi=None):
        return lambda: jnp.dot(h, w_ref[...] if lo is None else w_ref[:, lo:hi], preferred_element_type=F32)

    def finish_heads(out_ref, gain_row=None, rot=False, mul=None, means=False):
        def run(y):
            for g in range(n_grp):
                sl = slice(g * LANES, (g + 1) * LANES)
                t = y[:, sl]
                if gain_row is not None:
                    t = _head_pair_rms(t, gains_ref[gain_row:gain_row + 1, :])
                if rot:
                    t = _rotate(t, cos, sin)
                if means:
                    for sb in range(tm // MOBA_BLOCK):
                        kmean_ref[0, 0, sb:sb + 1, sl] = jnp.mean(
                            t[sb * MOBA_BLOCK:(sb + 1) * MOBA_BLOCK], axis=0, keepdims=True)
                if mul is not None:
                    t = t * mul
                out_ref[0, :, sl] = t.astype(BF16)
        return run

    def finish_forget(y):
        lf = jax.nn.log_sigmoid(y + bf_ref[...])
        row = lax.broadcasted_iota(jnp.int32, lf.shape, 0)
        step = 1
        while step < tm:
            lf = lf + jnp.where(row >= step, pltpu.roll(lf, step, axis=0), 0.0)
            step *= 2
        c = lf + c_carry[...]
        c_ref[0] = c
        c_carry[...] = c[tm - 1:tm, :]

    def finish_pool(u):
        ext = jnp.concatenate([pool_carry[...], u], axis=0)
        pos = j * tm + lax.broadcasted_iota(jnp.int32, (tm, 1), 0)
        pooled = []
        for g, w in enumerate(POOL_WINDOWS):
            sl = slice(g * POOL_GROUP, (g + 1) * POOL_GROUP)
            win = ext[:, sl]
            step = 1
            while step < w:
                win = win + pltpu.roll(win, step, axis=0)
                step *= 2
            cnt = jnp.minimum(pos + 1, w).astype(F32)
            pooled.append((win[POOL_HALO:, :] / cnt - u[:, sl]).astype(BF16))
        for g in range(len(POOL_WINDOWS)):
            sl = slice(g * POOL_GROUP, (g + 1) * POOL_GROUP)
            mixed = jnp.dot(pooled[g], pmix_ref[g], preferred_element_type=F32)
            ob_ref[0, :, sl] = (mixed * pscale_ref[:, sl]).astype(BF16)
        pool_carry[...] = u[tm - POOL_HALO:, :]

    def finish_memory(qm_all):
        heads = [slice(hd * MEM_HEAD_DIM, (hd + 1) * MEM_HEAD_DIM) for hd in range(MEM_HEADS)]
        scores = [lax.dot_general(_rms(qm_all[:, sl], mqg_ref[...]).astype(BF16), km_ref[0, :, sl], _NT,
                                  preferred_element_type=F32) * (MEM_HEAD_DIM ** -0.5) for sl in heads]
        probs = [jnp.exp(s - jnp.max(s, axis=-1, keepdims=True)) for s in scores]
        for sl, p in zip(heads, probs):
            den = jnp.sum(p, axis=-1, keepdims=True)
            o = jnp.dot(p.astype(BF16), vm_ref[0, :, sl], preferred_element_type=F32)
            om_ref[0, :, sl] = (o / den).astype(BF16)

    stages = [
        (project(wqkv_ref, 0 * width, 1 * width), finish_heads(qa_ref, 0, rot=True, mul=scale)),
        (project(wqkv_ref, 1 * width, 2 * width), finish_heads(ka_ref, 1, rot=True, means=True)),
        (project(wqkv_ref, 2 * width, 3 * width), finish_heads(va_ref)),
        (project(wqkv_ref, 3 * width, 4 * width), finish_heads(qc_ref, 2, mul=scale)),
        (project(wqkv_ref, 4 * width, 5 * width), finish_heads(kc_ref, 3)),
        (project(wqkv_ref, 5 * width, 6 * width), finish_heads(vc_ref)),
        (project(wf_ref), finish_forget),
        (project(wp_ref), finish_pool),
        (project(wmq_ref), finish_memory),
    ]
    pending = None
    for matmul, finish in stages:
        y = matmul()
        if pending is not None:
            pending[0](pending[1])
        pending = (finish, y)
    pending[0](pending[1])


def _project(x3, norm_gain, w_in, b_forget, moba_q_gain, moba_k_gain, fox_q_gain, fox_k_gain,
             cos, sin, pool_mix, pool_scale, k_m, v_m, mem_q_gain):
    b, s, d = x3.shape
    width = MOBA_HEADS * HEAD_DIM
    pool_width = POOL_GROUP * len(POOL_WINDOWS)
    mem_width = MEM_HEADS * MEM_HEAD_DIM
    m_len = k_m.shape[1]
    tm = min(TM_PROJ, s)
    nt = s // tm
    o0 = 6 * width
    w_qkv = w_in[:, :o0].astype(BF16)
    w_f = jnp.pad(w_in[:, o0:o0 + FOX_HEADS], ((0, 0), (0, LANES - FOX_HEADS))).astype(BF16)
    b_f = jnp.pad(b_forget, (0, LANES - FOX_HEADS))[None, :]
    o1 = o0 + FOX_HEADS
    w_pool = w_in[:, o1:o1 + pool_width].astype(BF16)
    o2 = o1 + pool_width
    w_mq = w_in[:, o2:o2 + mem_width].astype(BF16)
    gains = jnp.stack([jnp.tile(g, HEADS_PER_GROUP)
                       for g in (moba_q_gain, moba_k_gain, fox_q_gain, fox_k_gain)])

    tile = lambda w: pl.BlockSpec((1, tm, w), lambda i, j: (i, j, 0))
    act = lambda w: jax.ShapeDtypeStruct((b, s, w), BF16)
    outs = pl.pallas_call(
        _proj_kernel,
        out_shape=(act(width),) * 6 + (jax.ShapeDtypeStruct((b, s, LANES), F32), act(pool_width),
                                       act(mem_width), jax.ShapeDtypeStruct((b, nt, 8, width), F32)),
        grid=(b, nt),
        in_specs=[tile(d), _resident((1, d)), _resident((d, o0)), _resident((d, LANES)),
                  _resident((d, pool_width)), _resident((d, mem_width)), _resident((1, LANES)),
                  _resident((4, LANES)), tile(LANES), tile(LANES),
                  _resident((len(POOL_WINDOWS), POOL_GROUP, POOL_GROUP)), _resident((1, pool_width)),
                  pl.BlockSpec((1, m_len, mem_width), lambda i, j: (i, 0, 0)),
                  pl.BlockSpec((1, m_len, mem_width), lambda i, j: (i, 0, 0)),
                  _resident((1, MEM_HEAD_DIM))],
        out_specs=(tile(width),) * 6 + (tile(LANES), tile(pool_width), tile(mem_width),
                                        pl.BlockSpec((1, 1, 8, width), lambda i, j: (i, j, 0, 0))),
        scratch_shapes=[pltpu.VMEM((POOL_HALO, pool_width), F32), pltpu.VMEM((1, LANES), F32)],
        compiler_params=pltpu.CompilerParams(dimension_semantics=("arbitrary", "arbitrary"),
                                             vmem_limit_bytes=VMEM_LIMIT),
        name="mixer_projection",
    )(x3, norm_gain[None, :], w_qkv, w_f, w_pool, w_mq, b_f, gains, cos, sin,
      pool_mix.astype(BF16), pool_scale[None, :], k_m, v_m, mem_q_gain[None, :])
    qa, ka, va, qc, kc, vc, c, ob, om, kmean = outs
    kmean = kmean[:, :, :tm // MOBA_BLOCK, :].reshape(b, s // MOBA_BLOCK, width)
    return qa, ka, va, qc, kc, vc, c, ob, om, kmean


def _stage_values(v_ref, vt_ref):
    vt = v_ref[0].astype(F32).T
    s_len = vt.shape[1]
    row = lax.broadcasted_iota(jnp.int32, (AUG_ROWS, s_len), 0)
    tail = jnp.where(row == 0, 1.0, 0.0)
    for hh in range(HEADS_PER_GROUP):
        vt_ref[hh] = jnp.concatenate([vt[hh * HEAD_DIM:(hh + 1) * HEAD_DIM], tail], axis=0).astype(BF16)


def _stage_keys(k_ref, kaug_ref, r0, rows, ext0, ext1):
    k = k_ref[0, pl.ds(r0, rows), :]
    lane = lax.broadcasted_iota(jnp.int32, k.shape, 1)
    kaug_ref[0, pl.ds(r0, rows), :] = jnp.where(lane < HEAD_DIM, k, ext0.astype(BF16))
    kaug_ref[1, pl.ds(r0, rows), :] = jnp.where(lane >= HEAD_DIM, k, ext1.astype(BF16))


def _augment_queries(qt, e0, e1):
    tq = qt.shape[1]
    pad = jnp.zeros((HEAD_DIM - AUG_ROWS, tq), F32)
    qa0 = jnp.concatenate([qt[:HEAD_DIM], e0, pad], axis=0).astype(BF16)
    qa1 = jnp.concatenate([e1, pad, qt[HEAD_DIM:]], axis=0).astype(BF16)
    return qa0, qa1


def _issue_scores(qa, kaug_ref, s_ref, jb, slot):
    k0 = pl.multiple_of(jb * TK_ATTN, TK_ATTN)
    for hh in range(HEADS_PER_GROUP):
        s_ref[slot, hh] = jnp.dot(kaug_ref[hh, pl.ds(k0, TK_ATTN), :], qa[hh], preferred_element_type=F32)


def _attend_pair(qa, next_queries, kaug_ref, vt_ref, s_ref, i, o_ref):
    tq = o_ref.shape[0]
    tk = TK_ATTN
    key_i = lax.broadcasted_iota(jnp.int32, (tk, tq), 0)
    qry_i = lax.broadcasted_iota(jnp.int32, (tk, tq), 1)

    def issue_scores(jb, slot):
        _issue_scores(qa, kaug_ref, s_ref, jb, slot)

    def block_max(slot, diag=None):
        out = []
        for hh in range(HEADS_PER_GROUP):
            s = s_ref[slot, hh]
            if diag is not None:
                s = jnp.where(key_i + diag * tk <= qry_i, s, NEG_INF)
            out.append(jnp.max(s, axis=0, keepdims=True))
        return tuple(out)

    def accumulate(jb, slot, carry, mx, diag=None):
        k0 = pl.multiple_of(jb * tk, tk)
        new = []
        for hh in range(HEADS_PER_GROUP):
            m, acc = carry[hh]
            s = s_ref[slot, hh]
            if diag is not None:
                s = jnp.where(key_i + diag * tk <= qry_i, s, NEG_INF)
            m_new = jnp.maximum(m, mx[hh])
            alpha = jnp.exp2(m - m_new)
            p = jnp.exp2(s - m_new).astype(BF16)
            vb = vt_ref[hh, :, pl.ds(k0, tk)]
            new.append((m_new, alpha * acc + jnp.dot(vb, p, preferred_element_type=F32)))
        return tuple(new)

    def step(t, state):
        carry, mx = state
        issue_scores(2 * t + 1, 1)
        carry = accumulate(2 * t, 0, carry, mx)
        mx = block_max(1)
        issue_scores(2 * t + 2, 0)
        carry = accumulate(2 * t + 1, 1, carry, mx)
        return carry, block_max(0)

    init = tuple((jnp.full((1, tq), NEG_INF, F32), jnp.zeros((V_ROWS, tq), F32))
                 for _ in range(HEADS_PER_GROUP))
    state = lax.fori_loop(0, i // 2, lambda u, st: step(2 * u + 1, step(2 * u, st)), (init, block_max(0)))
    carry, _ = lax.cond(i % 2 == 1, lambda st: step(i - 1, st), lambda st: st, state)
    issue_scores(2 * i + 1, 1)
    qa_next = next_queries()
    carry = accumulate(2 * i, 0, carry, block_max(0, diag=0), diag=0)
    _issue_scores(qa_next, kaug_ref, s_ref, 0, 0)
    carry = accumulate(2 * i + 1, 1, carry, block_max(1, diag=1), diag=1)
    ot = jnp.concatenate([acc[:HEAD_DIM] / acc[HEAD_DIM:HEAD_DIM + 1] for (_, acc) in carry], axis=0)
    o_ref[...] = ot.T.astype(o_ref.dtype)
    return qa_next


def _split3(x):
    hi = x.astype(BF16).astype(F32)
    r = x - hi
    mid = r.astype(BF16).astype(F32)
    return hi, mid, r - mid


def _fox_kernel(q_ref, k_ref, v_ref, c_ref, crow_ref, o_ref, kaug_ref, vt_ref, s_ref):
    g = pl.program_id(1)
    tq = TQ_ATTN
    s_len = k_ref.shape[1]
    n_parts = 3

    _stage_values(v_ref, vt_ref)
    r = lax.broadcasted_iota(jnp.int32, (LANES, 2 * LANES), 0)
    col = lax.broadcasted_iota(jnp.int32, (LANES, 2 * LANES), 1)
    place = []
    for part in range(n_parts):
        hit = (((r == HEADS_PER_GROUP * g) & (col == HEAD_DIM + n_parts + part))
               | ((r == HEADS_PER_GROUP * g + 1) & (col == LANES + n_parts + part)))
        place.append(jnp.where(hit, -1.0, 0.0).astype(BF16))
    lane2 = lax.broadcasted_iota(jnp.int32, (1, 2 * LANES), 1)
    ones = jnp.where(((lane2 >= HEAD_DIM) & (lane2 < HEAD_DIM + n_parts))
                     | ((lane2 >= LANES) & (lane2 < LANES + n_parts)), 1.0, 0.0)
    rows = min(512, s_len)

    def chunk(ci, _):
        r0 = pl.multiple_of(ci * rows, rows)
        parts = _split3(c_ref[0, pl.ds(r0, rows), :] * LOG2E)
        ext = ones + sum(jnp.dot(p.astype(BF16), w, preferred_element_type=F32)
                         for p, w in zip(parts, place))
        _stage_keys(k_ref, kaug_ref, r0, rows, ext[:, :LANES], ext[:, LANES:])
        return 0

    lax.fori_loop(0, s_len // rows, chunk, 0)

    row = lax.broadcasted_iota(jnp.int32, (AUG_ROWS, tq), 0)

    def queries(i):
        q0 = pl.multiple_of(i * tq, tq)
        qt = q_ref[0, pl.ds(q0, tq), :].astype(F32).T
        exts = []
        for hh in range(HEADS_PER_GROUP):
            hi, mid, lo = _split3(crow_ref[0, 0, hh:hh + 1, pl.ds(q0, tq)] * LOG2E)
            e = jnp.where(row == 0, hi, jnp.where(row == 1, mid, jnp.where(row == 2, lo, 0.0)))
            exts.append(jnp.where((row >= n_parts) & (row < 2 * n_parts), 1.0, e))
        return _augment_queries(qt, *exts)

    _run_query_tiles(queries, kaug_ref, vt_ref, s_ref, o_ref, s_len // tq)


def _run_query_tiles(queries, kaug_ref, vt_ref, s_ref, o_ref, n_tiles):
    tq = TQ_ATTN

    def tile(i, qa):
        q0 = pl.multiple_of(i * tq, tq)
        nxt = lambda: queries(jnp.minimum(i + 1, n_tiles - 1))
        return _attend_pair(qa, nxt, kaug_ref, vt_ref, s_ref, i, o_ref.at[0, pl.ds(q0, tq), :])

    first = queries(0)
    _issue_scores(first, kaug_ref, s_ref, 0, 0)
    lax.fori_loop(0, n_tiles, tile, first)


def _attention_call(kernel_fn, name, q, k, v, extra_inputs, extra_specs):
    b, s, width = q.shape
    n_grp = width // LANES
    tq = TQ_ATTN
    assert tq == 2 * TK_ATTN and s % tq == 0
    seq = pl.BlockSpec((1, s, LANES), lambda bi, g: (bi, 0, g))
    return pl.pallas_call(
        kernel_fn,
        out_shape=jax.ShapeDtypeStruct((b, s, width), BF16),
        grid=(b, n_grp),
        in_specs=[seq, seq, seq] + extra_specs,
        out_specs=seq,
        scratch_shapes=[pltpu.VMEM((HEADS_PER_GROUP, s, LANES), BF16),
                        pltpu.VMEM((HEADS_PER_GROUP, V_ROWS, s), BF16),
                        pltpu.VMEM((2, HEADS_PER_GROUP, TK_ATTN, tq), F32)],
        compiler_params=_params(),
        name=name,
    )(q, k, v, *extra_inputs)


def _fox_attention(q, k, v, c):
    b, s, width = q.shape
    n_grp = width // LANES
    crow = c[:, :, :FOX_HEADS].transpose(0, 2, 1).reshape(b, n_grp, HEADS_PER_GROUP, s)
    return _attention_call(
        _fox_kernel, "forgetting_attention", q, k, v, (c, crow),
        [pl.BlockSpec((1, s, LANES), lambda bi, g: (bi, 0, 0)),
         pl.BlockSpec((1, 1, HEADS_PER_GROUP, s), lambda bi, g: (bi, g, 0, 0))])


def _moba_kernel(q_ref, k_ref, v_ref, kmean_ref, o_ref, kaug_ref, vt_ref, s_ref):
    tq = TQ_ATTN
    s_len = k_ref.shape[1]

    _stage_values(v_ref, vt_ref)
    rows = min(512, s_len)

    def chunk(ci, _):
        r0 = pl.multiple_of(ci * rows, rows)
        blk = (r0 + lax.broadcasted_iota(jnp.int32, (rows, LANES), 0)) // MOBA_BLOCK
        lane = lax.broadcasted_iota(jnp.int32, (rows, LANES), 1)
        _stage_keys(k_ref, kaug_ref, r0, rows,
                    jnp.where(lane - HEAD_DIM == blk, 1.0, 0.0), jnp.where(lane == blk, 1.0, 0.0))
        return 0

    lax.fori_loop(0, s_len // rows, chunk, 0)

    kmean = kmean_ref[0]
    km_hi = kmean.astype(BF16)
    km_lo = (kmean - km_hi.astype(F32)).astype(BF16)
    zeros = jnp.zeros((HEAD_DIM, tq), F32)
    row = lax.broadcasted_iota(jnp.int32, (AUG_ROWS, tq), 0)
    row_f = row.astype(F32)
    qpos = lax.broadcasted_iota(jnp.int32, (AUG_ROWS, tq), 1)

    def queries(i):
        q0 = pl.multiple_of(i * tq, tq)
        qt = q_ref[0, pl.ds(q0, tq), :].astype(F32).T
        own = (q0 + qpos) // MOBA_BLOCK
        past = row < own
        exts = []
        for hh in range(HEADS_PER_GROUP):
            qz = (jnp.concatenate([qt[:HEAD_DIM], zeros], axis=0) if hh == 0
                  else jnp.concatenate([zeros, qt[HEAD_DIM:]], axis=0)).astype(BF16)
            gate = (jnp.dot(km_hi, qz, preferred_element_type=F32)
                    + jnp.dot(km_lo, qz, preferred_element_type=F32))
            gate = jnp.where(past, gate, NEG_INF)
            keep = row == own
            for _ in range(MOBA_TOPK):
                top = jnp.max(gate, axis=0, keepdims=True)
                idx = jnp.min(jnp.where(gate == top, row_f, float(AUG_ROWS)), axis=0, keepdims=True)
                pick = row_f == idx
                keep = keep | (pick & past)
                gate = jnp.where(pick, -jnp.inf, gate)
            exts.append(jnp.where(keep, 0.0, NEG_INF))
        return _augment_queries(qt, *exts)

    _run_query_tiles(queries, kaug_ref, vt_ref, s_ref, o_ref, s_len // tq)


def _moba_attention(q, k, v, kmean):
    b, s, width = q.shape
    n_blk = s // MOBA_BLOCK
    assert n_blk <= AUG_ROWS and TK_ATTN == MOBA_BLOCK
    kmean = jnp.pad(kmean, ((0, 0), (0, AUG_ROWS - n_blk), (0, 0)))
    return _attention_call(
        _moba_kernel, "moba_attention", q, k, v, (kmean,),
        [pl.BlockSpec((1, AUG_ROWS, LANES), lambda bi, g: (bi, 0, g))])


def _merge_kernel(x_ref, gn_ref, oa_ref, ob_ref, oc_ref, om_ref, wg_ref, wbr_ref, wo_ref, o_ref):
    x = x_ref[...]
    d = x.shape[1]
    h = _rms(x, gn_ref[...]).astype(BF16)
    mix = jnp.zeros_like(x)
    for n, br in enumerate((oa_ref, ob_ref, oc_ref, om_ref)):
        gate = jax.nn.sigmoid(jnp.dot(h, wg_ref[:, n * d:(n + 1) * d], preferred_element_type=F32))
        y = jnp.dot(br[...], wbr_ref[n], preferred_element_type=F32)
        mix = mix + gate * y
    o_ref[...] = x + jnp.dot(mix.astype(BF16), wo_ref[...], preferred_element_type=F32)


def _merge(x2, norm_gain, branches, w_gate, w_branch, w_out):
    t, d = x2.shape
    bw = w_branch.shape[1]
    tm = min(TM_FFN, t)
    row = lambda w: pl.BlockSpec((tm, w), lambda i: (i, 0))
    return pl.pallas_call(
        _merge_kernel,
        out_shape=jax.ShapeDtypeStruct((t, d), F32),
        grid=(t // tm,),
        in_specs=[row(d), _resident((1, d)), row(bw), row(bw), row(bw), row(bw),
                  _resident((d, N_BRANCH * d)), _resident((N_BRANCH, bw, d)), _resident((d, d))],
        out_specs=row(d),
        compiler_params=_params(),
        name="gated_merge",
    )(x2, norm_gain[None, :], *branches, w_gate.astype(BF16), w_branch.astype(BF16), w_out.astype(BF16))


def kernel(x, mem, positions, norm_ffn1, w_ffn1_in, w_ffn1_out, norm_mix, w_in, b_forget, moba_q_gain, moba_k_gain, fox_q_gain, fox_k_gain, norm_mem, w_mem_kv, mem_q_gain, mem_k_gain, pool_mix, pool_scale, w_branch, w_out, norm_ffn2, w_ffn2_in, w_ffn2_out):
    b, s, d = x.shape
    depth = norm_ffn1.shape[0]
    gate_off = w_in.shape[2] - N_BRANCH * d
    cos, sin = _rope_tables(positions)
    x2 = x.reshape(b * s, d)
    for l in range(depth):
        x2 = _ffn(x2, norm_ffn1[l], w_ffn1_in[l], w_ffn1_out[l])
        k_m, v_m = _mem_kv(mem, norm_mem[l], w_mem_kv[l], mem_k_gain[l])
        qa, ka, va, qc, kc, vc, c, ob, om, kmean = _project(
            x2.reshape(b, s, d), norm_mix[l], w_in[l, :, :gate_off], b_forget[l],
            moba_q_gain[l], moba_k_gain[l], fox_q_gain[l], fox_k_gain[l],
            cos, sin, pool_mix[l], pool_scale[l], k_m, v_m, mem_q_gain[l])
        oa = _moba_attention(qa, ka, va, kmean)
        oc = _fox_attention(qc, kc, vc, c)
        flat = lambda a: a.reshape(b * s, a.shape[2])
        x2 = _merge(x2, norm_mix[l], (flat(oa), flat(ob), flat(oc), flat(om)),
                    w_in[l, :, gate_off:], w_branch[l], w_out[l])
        x2 = _ffn(x2, norm_ffn2[l], w_ffn2_in[l], w_ffn2_out[l])
    return x2.reshape(b, s, d)
```

```python
import jax
import jax.numpy as jnp
from jax import lax
from jax.experimental import pallas as pl
from jax.experimental.pallas import tpu as pltpu

HEAD_DIM = 64
MOBA_HEADS = 8
MOBA_BLOCK = 256
MOBA_TOPK = 3
FOX_HEADS = 8
POOL_WINDOWS = (2, 4, 8, 16)
POOL_GROUP = 128
MEM_HEADS = 4
MEM_HEAD_DIM = 128
ROPE_THETA = 500000.0
ROT_DIM = HEAD_DIM // 4
N_BRANCH = 4
RMS_EPS = 1e-6
NEG_INF = -1e30

LANES = 128
HEADS_PER_GROUP = LANES // HEAD_DIM
POOL_HALO = 16
VMEM_LIMIT = 56 * 1024 * 1024

TM_FFN = 512
TM_PROJ = 512
TQ_ATTN = 512
TK_ATTN = 256
AUG_ROWS = 16
FFN_CHUNK = 256

LOG2E = 1.4426950408889634
V_ROWS = HEAD_DIM + AUG_ROWS

F32 = jnp.float32
BF16 = jnp.bfloat16
_NT = (((1,), (1,)), ((), ()))


def _params():
    return pltpu.CompilerParams(vmem_limit_bytes=VMEM_LIMIT)


def _resident(shape):
    nd = len(shape)
    return pl.BlockSpec(shape, lambda *_: (0,) * nd, pipeline_mode=pl.Buffered(1))


def _rms(x, gain):
    ms = jnp.mean(x * x, axis=-1, keepdims=True)
    return x * lax.rsqrt(ms + RMS_EPS) * gain


def _head_pair_rms(x, gain):
    lane = lax.broadcasted_iota(jnp.int32, x.shape, 1)
    lo = lane < HEAD_DIM
    sq = x * x
    s_lo = jnp.sum(jnp.where(lo, sq, 0.0), axis=-1, keepdims=True)
    s_hi = jnp.sum(jnp.where(lo, 0.0, sq), axis=-1, keepdims=True)
    ms = jnp.where(lo, s_lo, s_hi) * (1.0 / HEAD_DIM)
    return x * lax.rsqrt(ms + RMS_EPS) * gain


def _rotate(x, cos, sin):
    lane = lax.broadcasted_iota(jnp.int32, x.shape, 1)
    half = ROT_DIM // 2
    fwd = pltpu.roll(x, LANES - half, axis=1)
    bwd = pltpu.roll(x, half, axis=1)
    partner = jnp.where((lane % HEAD_DIM) < half, fwd, bwd)
    return x * cos + partner * sin


def _rope_kernel(pos_ref, freq_ref, sign_ref, cos_ref, sin_ref):
    ang = freq_ref[...] * pos_ref[0].astype(F32)
    cos_r = jnp.cos(ang)
    sin_r = jnp.sin(ang) * sign_ref[...]
    ts = ang.shape[1]
    one = jnp.ones((HEAD_DIM - ROT_DIM, ts), F32)
    zero = jnp.zeros((HEAD_DIM - ROT_DIM, ts), F32)
    cos_ref[0] = jnp.concatenate([cos_r, one] * HEADS_PER_GROUP, axis=0).T
    sin_ref[0] = jnp.concatenate([sin_r, zero] * HEADS_PER_GROUP, axis=0).T


def _rope_tables(positions):
    b, s = positions.shape
    half = ROT_DIM // 2
    inv_freq = jnp.power(ROPE_THETA, -jnp.arange(half, dtype=F32) * (2.0 / ROT_DIM))
    freq = jnp.tile(inv_freq, 2)[:, None]
    sign = jnp.concatenate([-jnp.ones((half,), F32), jnp.ones((half,), F32)])[:, None]
    ts = min(s, 1024)
    return pl.pallas_call(
        _rope_kernel,
        out_shape=(jax.ShapeDtypeStruct((b, s, LANES), F32),) * 2,
        grid=(b, s // ts),
        in_specs=[pl.BlockSpec((1, 1, ts), lambda i, j: (i, 0, j)),
                  pl.BlockSpec((ROT_DIM, 1), lambda i, j: (0, 0)),
                  pl.BlockSpec((ROT_DIM, 1), lambda i, j: (0, 0))],
        out_specs=(pl.BlockSpec((1, ts, LANES), lambda i, j: (i, j, 0)),) * 2,
        compiler_params=_params(),
        name="rope_tables",
    )(positions.reshape(b, 1, s), freq, sign)


def _ffn_kernel(x_ref, g_ref, wg_ref, wu_ref, wo_ref, o_ref):
    x = x_ref[...]
    h = _rms(x, g_ref[...]).astype(BF16)
    d_ff = wg_ref.shape[1]
    acc = jnp.zeros_like(x)
    for c in range(d_ff // FFN_CHUNK):
        sl = slice(c * FFN_CHUNK, (c + 1) * FFN_CHUNK)
        g = jnp.dot(h, wg_ref[:, sl], preferred_element_type=F32)
        u = jnp.dot(h, wu_ref[:, sl], preferred_element_type=F32)
        a = (g * jax.nn.sigmoid(g) * u).astype(BF16)
        acc = acc + jnp.dot(a, wo_ref[sl, :], preferred_element_type=F32)
    o_ref[...] = x + 0.5 * acc


def _ffn(x2, gain, w_in, w_out):
    t, d = x2.shape
    d_ff = w_out.shape[0]
    wg = w_in[:, :d_ff].astype(BF16)
    wu = w_in[:, d_ff:].astype(BF16)
    wo = w_out.astype(BF16)
    tm = min(TM_FFN, t)
    return pl.pallas_call(
        _ffn_kernel,
        out_shape=jax.ShapeDtypeStruct((t, d), F32),
        grid=(t // tm,),
        in_specs=[pl.BlockSpec((tm, d), lambda i: (i, 0)),
                  _resident((1, d)), _resident((d, d_ff)), _resident((d, d_ff)), _resident((d_ff, d))],
        out_specs=pl.BlockSpec((tm, d), lambda i: (i, 0)),
        compiler_params=_params(),
        name="ffn_half_step",
    )(x2, gain[None, :], wg, wu, wo)


def _memkv_kernel(mem_ref, gn_ref, w_ref, kg_ref, k_ref, v_ref):
    width = k_ref.shape[2]
    mn = _rms(mem_ref[0], gn_ref[...]).astype(BF16)
    kv = jnp.dot(mn, w_ref[...], preferred_element_type=F32)
    for h in range(MEM_HEADS):
        sl = slice(h * MEM_HEAD_DIM, (h + 1) * MEM_HEAD_DIM)
        k_ref[0, :, sl] = _rms(kv[:, sl], kg_ref[...]).astype(BF16)
    v_ref[0] = kv[:, width:].astype(BF16)


def _mem_kv(mem, norm_gain, w_kv, k_gain):
    b, m, d = mem.shape
    width = MEM_HEADS * MEM_HEAD_DIM
    return pl.pallas_call(
        _memkv_kernel,
        out_shape=(jax.ShapeDtypeStruct((b, m, width), BF16),) * 2,
        grid=(b,),
        in_specs=[pl.BlockSpec((1, m, d), lambda i: (i, 0, 0)),
                  _resident((1, d)), _resident((d, 2 * width)), _resident((1, MEM_HEAD_DIM))],
        out_specs=(pl.BlockSpec((1, m, width), lambda i: (i, 0, 0)),) * 2,
        compiler_params=_params(),
        name="memory_kv",
    )(mem, norm_gain[None, :], w_kv.astype(BF16), k_gain[None, :])


def _proj_kernel(x_ref, gn_ref, wqkv_ref, wf_ref, wp_ref, wmq_ref, bf_ref, gains_ref,
                 cos_ref, sin_ref, pmix_ref, pscale_ref, km_ref, vm_ref, mqg_ref,
                 qa_ref, ka_ref, va_ref, qc_ref, kc_ref, vc_ref, c_ref, ob_ref, om_ref, kmean_ref,
                 pool_carry, c_carry):
    j = pl.program_id(1)
    tm = x_ref.shape[1]
    width = qa_ref.shape[2]
    n_grp = width // LANES

    @pl.when(j == 0)
    def _():
        pool_carry[...] = jnp.zeros_like(pool_carry)
        c_carry[...] = jnp.zeros_like(c_carry)

    h = _rms(x_ref[0], gn_ref[...]).astype(BF16)
    scale = HEAD_DIM ** -0.5 * LOG2E
    cos = cos_ref[0]
    sin = sin_ref[0]
    kmean_ref[...] = jnp.zeros_like(kmean_ref)

    def project(w_ref, lo=None, hi=None):
        return lambda: jnp.dot(h, w_ref[...] if lo is None else w_ref[:, lo:hi], preferred_element_type=F32)

    def finish_heads(out_ref, gain_row=None, rot=False, mul=None, means=False):
        def run(y):
            for g in range(n_grp):
                sl = slice(g * LANES, (g + 1) * LANES)
                t = y[:, sl]
                if gain_row is not None:
                    t = _head_pair_rms(t, gains_ref[gain_row:gain_row + 1, :])
                if rot:
                    t = _rotate(t, cos, sin)
                if means:
                    for sb in range(tm // MOBA_BLOCK):
                        kmean_ref[0, 0, sb:sb + 1, sl] = jnp.mean(
                            t[sb * MOBA_BLOCK:(sb + 1) * MOBA_BLOCK], axis=0, keepdims=True)
                if mul is not None:
                    t = t * mul
                out_ref[0, :, sl] = t.astype(BF16)
        return run

    def finish_forget(y):
        lf = jax.nn.log_sigmoid(y + bf_ref[...])
        row = lax.broadcasted_iota(jnp.int32, lf.shape, 0)
        step = 1
        while step < tm:
            lf = lf + jnp.where(row >= step, pltpu.roll(lf, step, axis=0), 0.0)
            step *= 2
        c = lf + c_carry[...]
        c_ref[0] = c
        c_carry[...] = c[tm - 1:tm, :]

    def finish_pool(u):
        ext = jnp.concatenate([pool_carry[...], u], axis=0)
        pos = j * tm + lax.broadcasted_iota(jnp.int32, (tm, 1), 0)
        pooled = []
        for g, w in enumerate(POOL_WINDOWS):
            sl = slice(g * POOL_GROUP, (g + 1) * POOL_GROUP)
            win = ext[:, sl]
            step = 1
            while step < w:
                win = win + pltpu.roll(win, step, axis=0)
                step *= 2
            cnt = jnp.minimum(pos + 1, w).astype(F32)
            pooled.append((win[POOL_HALO:, :] / cnt - u[:, sl]).astype(BF16))
        for g in range(len(POOL_WINDOWS)):
            sl = slice(g * POOL_GROUP, (g + 1) * POOL_GROUP)
            mixed = jnp.dot(pooled[g], pmix_ref[g], preferred_element_type=F32)
            ob_ref[0, :, sl] = (mixed * pscale_ref[:, sl]).astype(BF16)
        pool_carry[...] = u[tm - POOL_HALO:, :]

    def finish_memory(qm_all):
        heads = [slice(hd * MEM_HEAD_DIM, (hd + 1) * MEM_HEAD_DIM) for hd in range(MEM_HEADS)]
        scores = [lax.dot_general(_rms(qm_all[:, sl], mqg_ref[...]).astype(BF16), km_ref[0, :, sl], _NT,
                                  preferred_element_type=F32) * (MEM_HEAD_DIM ** -0.5) for sl in heads]
        probs = [jnp.exp(s - jnp.max(s, axis=-1, keepdims=True)) for s in scores]
        for sl, p in zip(heads, probs):
            den = jnp.sum(p, axis=-1, keepdims=True)
            o = jnp.dot(p.astype(BF16), vm_ref[0, :, sl], preferred_element_type=F32)
            om_ref[0, :, sl] = (o / den).astype(BF16)

    stages = [
        (project(wqkv_ref, 0 * width, 1 * width), finish_heads(qa_ref, 0, rot=True, mul=scale)),
        (project(wqkv_ref, 1 * width, 2 * width), finish_heads(ka_ref, 1, rot=True, means=True)),
        (project(wqkv_ref, 2 * width, 3 * width), finish_heads(va_ref)),
        (project(wqkv_ref, 3 * width, 4 * width), finish_heads(qc_ref, 2, mul=scale)),
        (project(wqkv_ref, 4 * width, 5 * width), finish_heads(kc_ref, 3)),
        (project(wqkv_ref, 5 * width, 6 * width), finish_heads(vc_ref)),
        (project(wf_ref), finish_forget),
        (project(wp_ref), finish_pool),
        (project(wmq_ref), finish_memory),
    ]
    pending = None
    for matmul, finish in stages:
        y = matmul()
        if pending is not None:
            pending[0](pending[1])
        pending = (finish, y)
    pending[0](pending[1])


def _project(x3, norm_gain, w_in, b_forget, moba_q_gain, moba_k_gain, fox_q_gain, fox_k_gain,
             cos, sin, pool_mix, pool_scale, k_m, v_m, mem_q_gain):
    b, s, d = x3.shape
    width = MOBA_HEADS * HEAD_DIM
    pool_width = POOL_GROUP * len(POOL_WINDOWS)
    mem_width = MEM_HEADS * MEM_HEAD_DIM
    m_len = k_m.shape[1]
    tm = min(TM_PROJ, s)
    nt = s // tm
    o0 = 6 * width
    w_qkv = w_in[:, :o0].astype(BF16)
    w_f = jnp.pad(w_in[:, o0:o0 + FOX_HEADS], ((0, 0), (0, LANES - FOX_HEADS))).astype(BF16)
    b_f = jnp.pad(b_forget, (0, LANES - FOX_HEADS))[None, :]
    o1 = o0 + FOX_HEADS
    w_pool = w_in[:, o1:o1 + pool_width].astype(BF16)
    o2 = o1 + pool_width
    w_mq = w_in[:, o2:o2 + mem_width].astype(BF16)
    gains = jnp.stack([jnp.tile(g, HEADS_PER_GROUP)
                       for g in (moba_q_gain, moba_k_gain, fox_q_gain, fox_k_gain)])

    tile = lambda w: pl.BlockSpec((1, tm, w), lambda i, j: (i, j, 0))
    act = lambda w: jax.ShapeDtypeStruct((b, s, w), BF16)
    outs = pl.pallas_call(
        _proj_kernel,
        out_shape=(act(width),) * 6 + (jax.ShapeDtypeStruct((b, s, LANES), F32), act(pool_width),
                                       act(mem_width), jax.ShapeDtypeStruct((b, nt, 8, width), F32)),
        grid=(b, nt),
        in_specs=[tile(d), _resident((1, d)), _resident((d, o0)), _resident((d, LANES)),
                  _resident((d, pool_width)), _resident((d, mem_width)), _resident((1, LANES)),
                  _resident((4, LANES)), tile(LANES), tile(LANES),
                  _resident((len(POOL_WINDOWS), POOL_GROUP, POOL_GROUP)), _resident((1, pool_width)),
                  pl.BlockSpec((1, m_len, mem_width), lambda i, j: (i, 0, 0)),
                  pl.BlockSpec((1, m_len, mem_width), lambda i, j: (i, 0, 0)),
                  _resident((1, MEM_HEAD_DIM))],
        out_specs=(tile(width),) * 6 + (tile(LANES), tile(pool_width), tile(mem_width),
                                        pl.BlockSpec((1, 1, 8, width), lambda i, j: (i, j, 0, 0))),
        scratch_shapes=[pltpu.VMEM((POOL_HALO, pool_width), F32), pltpu.VMEM((1, LANES), F32)],
        compiler_params=pltpu.CompilerParams(dimension_semantics=("arbitrary", "arbitrary"),
                                             vmem_limit_bytes=VMEM_LIMIT),
        name="mixer_projection",
    )(x3, norm_gain[None, :], w_qkv, w_f, w_pool, w_mq, b_f, gains, cos, sin,
      pool_mix.astype(BF16), pool_scale[None, :], k_m, v_m, mem_q_gain[None, :])
    qa, ka, va, qc, kc, vc, c, ob, om, kmean = outs
    kmean = kmean[:, :, :tm // MOBA_BLOCK, :].reshape(b, s // MOBA_BLOCK, width)
    return qa, ka, va, qc, kc, vc, c, ob, om, kmean


def _stage_values(v_ref, vt_ref):
    vt = v_ref[0].T
    s_len = vt.shape[1]
    row = lax.broadcasted_iota(jnp.int32, (AUG_ROWS, s_len), 0)
    tail = jnp.where(row == 0, 1.0, 0.0).astype(BF16)
    for hh in range(HEADS_PER_GROUP):
        vt_ref[hh] = jnp.concatenate([vt[hh * HEAD_DIM:(hh + 1) * HEAD_DIM], tail], axis=0)


def _stage_keys(k_ref, kaug_ref, r0, rows, ext0, ext1):
    k = k_ref[0, pl.ds(r0, rows), :]
    lane = lax.broadcasted_iota(jnp.int32, k.shape, 1)
    kaug_ref[0, pl.ds(r0, rows), :] = jnp.where(lane < HEAD_DIM, k, ext0.astype(BF16))
    kaug_ref[1, pl.ds(r0, rows), :] = jnp.where(lane >= HEAD_DIM, k, ext1.astype(BF16))


def _augment_queries(qt, e0, e1):
    tq = qt.shape[1]
    pad = jnp.zeros((HEAD_DIM - AUG_ROWS, tq), F32)
    qa0 = jnp.concatenate([qt[:HEAD_DIM], e0, pad], axis=0).astype(BF16)
    qa1 = jnp.concatenate([e1, pad, qt[HEAD_DIM:]], axis=0).astype(BF16)
    return qa0, qa1


def _issue_scores(qa, kaug_ref, s_ref, jb, slot):
    k0 = pl.multiple_of(jb * TK_ATTN, TK_ATTN)
    for hh in range(HEADS_PER_GROUP):
        s_ref[slot, hh] = jnp.dot(kaug_ref[hh, pl.ds(k0, TK_ATTN), :], qa[hh], preferred_element_type=F32)


def _attend_pair(qa, next_queries, kaug_ref, vt_ref, s_ref, i, o_ref):
    tq = o_ref.shape[0]
    tk = TK_ATTN
    key_i = lax.broadcasted_iota(jnp.int32, (tk, tq), 0)
    qry_i = lax.broadcasted_iota(jnp.int32, (tk, tq), 1)

    def issue_scores(jb, slot):
        _issue_scores(qa, kaug_ref, s_ref, jb, slot)

    def block_max(slot, diag=None):
        out = []
        for hh in range(HEADS_PER_GROUP):
            s = s_ref[slot, hh]
            if diag is not None:
                s = jnp.where(key_i + diag * tk <= qry_i, s, NEG_INF)
            out.append(jnp.max(s, axis=0, keepdims=True))
        return tuple(out)

    def accumulate(jb, slot, carry, mx, diag=None):
        k0 = pl.multiple_of(jb * tk, tk)
        new = []
        for hh in range(HEADS_PER_GROUP):
            m, acc = carry[hh]
            s = s_ref[slot, hh]
            if diag is not None:
                s = jnp.where(key_i + diag * tk <= qry_i, s, NEG_INF)
            m_new = jnp.maximum(m, mx[hh])
            alpha = jnp.exp2(m - m_new)
            p = jnp.exp2(s - m_new).astype(BF16)
            vb = vt_ref[hh, :, pl.ds(k0, tk)]
            new.append((m_new, alpha * acc + jnp.dot(vb, p, preferred_element_type=F32)))
        return tuple(new)

    def step(t, state):
        carry, mx = state
        issue_scores(2 * t + 1, 1)
        carry = accumulate(2 * t, 0, carry, mx)
        mx = block_max(1)
        issue_scores(2 * t + 2, 0)
        carry = accumulate(2 * t + 1, 1, carry, mx)
        return carry, block_max(0)

    init = tuple((jnp.full((1, tq), NEG_INF, F32), jnp.zeros((V_ROWS, tq), F32))
                 for _ in range(HEADS_PER_GROUP))
    state = lax.fori_loop(0, i // 2, lambda u, st: step(2 * u + 1, step(2 * u, st)), (init, block_max(0)))
    carry, _ = lax.cond(i % 2 == 1, lambda st: step(i - 1, st), lambda st: st, state)
    issue_scores(2 * i + 1, 1)
    qa_next = next_queries()
    carry = accumulate(2 * i, 0, carry, block_max(0, diag=0), diag=0)
    _issue_scores(qa_next, kaug_ref, s_ref, 0, 0)
    carry = accumulate(2 * i + 1, 1, carry, block_max(1, diag=1), diag=1)
    ot = jnp.concatenate([acc[:HEAD_DIM] / acc[HEAD_DIM:HEAD_DIM + 1] for (_, acc) in carry], axis=0)
    o_ref[...] = ot.T.astype(o_ref.dtype)
    return qa_next


def _split3(x):
    hi = x.astype(BF16).astype(F32)
    r = x - hi
    mid = r.astype(BF16).astype(F32)
    return hi, mid, r - mid


def _fox_kernel(q_ref, k_ref, v_ref, c_ref, crow_ref, o_ref, kaug_ref, vt_ref, s_ref):
    g = pl.program_id(1)
    tq = TQ_ATTN
    s_len = k_ref.shape[1]
    n_parts = 3

    _stage_values(v_ref, vt_ref)
    r = lax.broadcasted_iota(jnp.int32, (n_parts * LANES, 2 * LANES), 0)
    col = lax.broadcasted_iota(jnp.int32, (n_parts * LANES, 2 * LANES), 1)
    hit = jnp.zeros(r.shape, jnp.bool_)
    for part in range(n_parts):
        hit = hit | (((r == part * LANES + HEADS_PER_GROUP * g) & (col == HEAD_DIM + n_parts + part))
                     | ((r == part * LANES + HEADS_PER_GROUP * g + 1) & (col == LANES + n_parts + part)))
    place = jnp.where(hit, -1.0, 0.0).astype(BF16)
    lane2 = lax.broadcasted_iota(jnp.int32, (1, 2 * LANES), 1)
    ones = jnp.where(((lane2 >= HEAD_DIM) & (lane2 < HEAD_DIM + n_parts))
                     | ((lane2 >= LANES) & (lane2 < LANES + n_parts)), 1.0, 0.0)
    rows = min(512, s_len)

    def chunk(ci, _):
        r0 = pl.multiple_of(ci * rows, rows)
        parts = jnp.concatenate([p.astype(BF16) for p in _split3(c_ref[0, pl.ds(r0, rows), :] * LOG2E)], axis=1)
        ext = ones + jnp.dot(parts, place, preferred_element_type=F32)
        _stage_keys(k_ref, kaug_ref, r0, rows, ext[:, :LANES], ext[:, LANES:])
        return 0

    lax.fori_loop(0, s_len // rows, chunk, 0)

    row = lax.broadcasted_iota(jnp.int32, (AUG_ROWS, tq), 0)

    def queries(i):
        q0 = pl.multiple_of(i * tq, tq)
        qt = q_ref[0, pl.ds(q0, tq), :].astype(F32).T
        exts = []
        for hh in range(HEADS_PER_GROUP):
            hi, mid, lo = _split3(crow_ref[0, 0, hh:hh + 1, pl.ds(q0, tq)] * LOG2E)
            e = jnp.where(row == 0, hi, jnp.where(row == 1, mid, jnp.where(row == 2, lo, 0.0)))
            exts.append(jnp.where((row >= n_parts) & (row < 2 * n_parts), 1.0, e))
        return _augment_queries(qt, *exts)

    _run_query_tiles(queries, kaug_ref, vt_ref, s_ref, o_ref, s_len // tq)


def _run_query_tiles(queries, kaug_ref, vt_ref, s_ref, o_ref, n_tiles):
    tq = TQ_ATTN

    def tile(i, qa):
        q0 = pl.multiple_of(i * tq, tq)
        nxt = lambda: queries(jnp.minimum(i + 1, n_tiles - 1))
        return _attend_pair(qa, nxt, kaug_ref, vt_ref, s_ref, i, o_ref.at[0, pl.ds(q0, tq), :])

    first = queries(0)
    _issue_scores(first, kaug_ref, s_ref, 0, 0)
    lax.fori_loop(0, n_tiles, tile, first)


def _attention_call(kernel_fn, name, q, k, v, extra_inputs, extra_specs):
    b, s, width = q.shape
    n_grp = width // LANES
    tq = TQ_ATTN
    assert tq == 2 * TK_ATTN and s % tq == 0
    seq = pl.BlockSpec((1, s, LANES), lambda bi, g: (bi, 0, g))
    return pl.pallas_call(
        kernel_fn,
        out_shape=jax.ShapeDtypeStruct((b, s, width), BF16),
        grid=(b, n_grp),
        in_specs=[seq, seq, seq] + extra_specs,
        out_specs=seq,
        scratch_shapes=[pltpu.VMEM((HEADS_PER_GROUP, s, LANES), BF16),
                        pltpu.VMEM((HEADS_PER_GROUP, V_ROWS, s), BF16),
                        pltpu.VMEM((2, HEADS_PER_GROUP, TK_ATTN, tq), F32)],
        compiler_params=_params(),
        name=name,
    )(q, k, v, *extra_inputs)


def _fox_attention(q, k, v, c):
    b, s, width = q.shape
    n_grp = width // LANES
    crow = c[:, :, :FOX_HEADS].transpose(0, 2, 1).reshape(b, n_grp, HEADS_PER_GROUP, s)
    return _attention_call(
        _fox_kernel, "forgetting_attention", q, k, v, (c, crow),
        [pl.BlockSpec((1, s, LANES), lambda bi, g: (bi, 0, 0)),
         pl.BlockSpec((1, 1, HEADS_PER_GROUP, s), lambda bi, g: (bi, g, 0, 0))])


def _moba_kernel(q_ref, k_ref, v_ref, kmean_ref, o_ref, kaug_ref, vt_ref, s_ref):
    tq = TQ_ATTN
    s_len = k_ref.shape[1]

    _stage_values(v_ref, vt_ref)
    lane_k = lax.broadcasted_iota(jnp.int32, (MOBA_BLOCK, LANES), 1)
    lane_1 = lax.broadcasted_iota(jnp.int32, (1, LANES), 1)

    def stage_block(n, _):
        r0 = pl.multiple_of(n * MOBA_BLOCK, MOBA_BLOCK)
        k = k_ref[0, pl.ds(r0, MOBA_BLOCK), :]
        hot_hi = jnp.where(lane_1 == n + HEAD_DIM, 1.0, 0.0).astype(BF16)
        hot_lo = jnp.where(lane_1 == n, 1.0, 0.0).astype(BF16)
        kaug_ref[0, pl.ds(r0, MOBA_BLOCK), :] = jnp.where(lane_k < HEAD_DIM, k, hot_hi)
        kaug_ref[1, pl.ds(r0, MOBA_BLOCK), :] = jnp.where(lane_k >= HEAD_DIM, k, hot_lo)
        return 0

    lax.fori_loop(0, s_len // MOBA_BLOCK, stage_block, 0)

    kmean = kmean_ref[0]
    km_hi = kmean.astype(BF16)
    km_lo = (kmean - km_hi.astype(F32)).astype(BF16)
    zeros = jnp.zeros((HEAD_DIM, tq), F32)
    row = lax.broadcasted_iota(jnp.int32, (AUG_ROWS, tq), 0)
    row_f = row.astype(F32)
    qpos = lax.broadcasted_iota(jnp.int32, (AUG_ROWS, tq), 1)

    def queries(i):
        q0 = pl.multiple_of(i * tq, tq)
        qt = q_ref[0, pl.ds(q0, tq), :].astype(F32).T
        own = (q0 + qpos) // MOBA_BLOCK
        past = row < own
        exts = []
        for hh in range(HEADS_PER_GROUP):
            qz = (jnp.concatenate([qt[:HEAD_DIM], zeros], axis=0) if hh == 0
                  else jnp.concatenate([zeros, qt[HEAD_DIM:]], axis=0)).astype(BF16)
            gate = (jnp.dot(km_hi, qz, preferred_element_type=F32)
                    + jnp.dot(km_lo, qz, preferred_element_type=F32))
            gate = jnp.where(past, gate, NEG_INF)
            keep = row == own
            for _ in range(MOBA_TOPK):
                top = jnp.max(gate, axis=0, keepdims=True)
                idx = jnp.min(jnp.where(gate == top, row_f, float(AUG_ROWS)), axis=0, keepdims=True)
                pick = row_f == idx
                keep = keep | (pick & past)
                gate = jnp.where(pick, -jnp.inf, gate)
            exts.append(jnp.where(keep, 0.0, NEG_INF))
        return _augment_queries(qt, *exts)

    _run_query_tiles(queries, kaug_ref, vt_ref, s_ref, o_ref, s_len // tq)


def _moba_attention(q, k, v, kmean):
    b, s, width = q.shape
    n_blk = s // MOBA_BLOCK
    assert n_blk <= AUG_ROWS and TK_ATTN == MOBA_BLOCK
    kmean = jnp.pad(kmean, ((0, 0), (0, AUG_ROWS - n_blk), (0, 0)))
    return _attention_call(
        _moba_kernel, "moba_attention", q, k, v, (kmean,),
        [pl.BlockSpec((1, AUG_ROWS, LANES), lambda bi, g: (bi, 0, g))])


def _merge_kernel(x_ref, gn_ref, oa_ref, ob_ref, oc_ref, om_ref, wg_ref, wbr_ref, wo_ref, o_ref):
    x = x_ref[...]
    d = x.shape[1]
    h = _rms(x, gn_ref[...]).astype(BF16)
    mix = jnp.zeros_like(x)
    for n, br in enumerate((oa_ref, ob_ref, oc_ref, om_ref)):
        gate = jax.nn.sigmoid(jnp.dot(h, wg_ref[:, n * d:(n + 1) * d], preferred_element_type=F32))
        y = jnp.dot(br[...], wbr_ref[n], preferred_element_type=F32)
        mix = mix + gate * y
    o_ref[...] = x + jnp.dot(mix.astype(BF16), wo_ref[...], preferred_element_type=F32)


def _merge(x2, norm_gain, branches, w_gate, w_branch, w_out):
    t, d = x2.shape
    bw = w_branch.shape[1]
    tm = min(TM_FFN, t)
    row = lambda w: pl.BlockSpec((tm, w), lambda i: (i, 0))
    return pl.pallas_call(
        _merge_kernel,
        out_shape=jax.ShapeDtypeStruct((t, d), F32),
        grid=(t // tm,),
        in_specs=[row(d), _resident((1, d)), row(bw), row(bw), row(bw), row(bw),
                  _resident((d, N_BRANCH * d)), _resident((N_BRANCH, bw, d)), _resident((d, d))],
        out_specs=row(d),
        compiler_params=_params(),
        name="gated_merge",
    )(x2, norm_gain[None, :], *branches, w_gate.astype(BF16), w_branch.astype(BF16), w_out.astype(BF16))


def kernel(x, mem, positions, norm_ffn1, w_ffn1_in, w_ffn1_out, norm_mix, w_in, b_forget, moba_q_gain, moba_k_gain, fox_q_gain, fox_k_gain, norm_mem, w_mem_kv, mem_q_gain, mem_k_gain, pool_mix, pool_scale, w_branch, w_out, norm_ffn2, w_ffn2_in, w_ffn2_out):
    b, s, d = x.shape
    depth = norm_ffn1.shape[0]
    gate_off = w_in.shape[2] - N_BRANCH * d
    cos, sin = _rope_tables(positions)
    x2 = x.reshape(b * s, d)
    for l in range(depth):
        x2 = _ffn(x2, norm_ffn1[l], w_ffn1_in[l], w_ffn1_out[l])
        k_m, v_m = _mem_kv(mem, norm_mem[l], w_mem_kv[l], mem_k_gain[l])
        qa, ka, va, qc, kc, vc, c, ob, om, kmean = _project(
            x2.reshape(b, s, d), norm_mix[l], w_in[l, :, :gate_off], b_forget[l],
            moba_q_gain[l], moba_k_gain[l], fox_q_gain[l], fox_k_gain[l],
            cos, sin, pool_mix[l], pool_scale[l], k_m, v_m, mem_q_gain[l])
        oa = _moba_attention(qa, ka, va, kmean)
        oc = _fox_attention(qc, kc, vc, c)
        flat = lambda a: a.reshape(b * s, a.shape[2])
        x2 = _merge(x2, norm_mix[l], (flat(oa), flat(ob), flat(oc), flat(om)),
                    w_in[l, :, gate_off:], w_branch[l], w_out[l])
        x2 = _ffn(x2, norm_ffn2[l], w_ffn2_in[l], w_ffn2_out[l])
    return x2.reshape(b, s, d)
```

```python
import jax
import jax.numpy as jnp
from jax import lax
from jax.experimental import pallas as pl
from jax.experimental.pallas import tpu as pltpu

HEAD_DIM = 64
MOBA_HEADS = 8
MOBA_BLOCK = 256
MOBA_TOPK = 3
FOX_HEADS = 8
POOL_WINDOWS = (2, 4, 8, 16)
POOL_GROUP = 128
MEM_HEADS = 4
MEM_HEAD_DIM = 128
ROPE_THETA = 500000.0
ROT_DIM = HEAD_DIM // 4
N_BRANCH = 4
RMS_EPS = 1e-6
NEG_INF = -1e30

LANES = 128
HEADS_PER_GROUP = LANES // HEAD_DIM
POOL_HALO = 16
VMEM_LIMIT = 56 * 1024 * 1024

TM_FFN = 1024
TM_PROJ = 512
TQ_ATTN = 512
TK_ATTN = 256
AUG_ROWS = 16
FFN_CHUNK = 256

LOG2E = 1.4426950408889634
V_ROWS = HEAD_DIM + AUG_ROWS

F32 = jnp.float32
BF16 = jnp.bfloat16
_NT = (((1,), (1,)), ((), ()))


def _params():
    return pltpu.CompilerParams(vmem_limit_bytes=VMEM_LIMIT)


def _resident(shape):
    nd = len(shape)
    return pl.BlockSpec(shape, lambda *_: (0,) * nd, pipeline_mode=pl.Buffered(1))


def _rms(x, gain):
    ms = jnp.mean(x * x, axis=-1, keepdims=True)
    return x * lax.rsqrt(ms + RMS_EPS) * gain


def _head_pair_rms(x, gain):
    lane = lax.broadcasted_iota(jnp.int32, x.shape, 1)
    lo = lane < HEAD_DIM
    sq = x * x
    s_lo = jnp.sum(jnp.where(lo, sq, 0.0), axis=-1, keepdims=True)
    s_hi = jnp.sum(jnp.where(lo, 0.0, sq), axis=-1, keepdims=True)
    ms = jnp.where(lo, s_lo, s_hi) * (1.0 / HEAD_DIM)
    return x * lax.rsqrt(ms + RMS_EPS) * gain


def _rotate(x, cos, sin):
    lane = lax.broadcasted_iota(jnp.int32, x.shape, 1)
    half = ROT_DIM // 2
    fwd = pltpu.roll(x, LANES - half, axis=1)
    bwd = pltpu.roll(x, half, axis=1)
    partner = jnp.where((lane % HEAD_DIM) < half, fwd, bwd)
    return x * cos + partner * sin


def _rope_kernel(pos_ref, freq_ref, sign_ref, cos_ref, sin_ref):
    ang = freq_ref[...] * pos_ref[0].astype(F32)
    cos_r = jnp.cos(ang)
    sin_r = jnp.sin(ang) * sign_ref[...]
    ts = ang.shape[1]
    one = jnp.ones((HEAD_DIM - ROT_DIM, ts), F32)
    zero = jnp.zeros((HEAD_DIM - ROT_DIM, ts), F32)
    cos_ref[0] = jnp.concatenate([cos_r, one] * HEADS_PER_GROUP, axis=0).T
    sin_ref[0] = jnp.concatenate([sin_r, zero] * HEADS_PER_GROUP, axis=0).T


def _rope_tables(positions):
    b, s = positions.shape
    half = ROT_DIM // 2
    inv_freq = jnp.power(ROPE_THETA, -jnp.arange(half, dtype=F32) * (2.0 / ROT_DIM))
    freq = jnp.tile(inv_freq, 2)[:, None]
    sign = jnp.concatenate([-jnp.ones((half,), F32), jnp.ones((half,), F32)])[:, None]
    ts = min(s, 1024)
    return pl.pallas_call(
        _rope_kernel,
        out_shape=(jax.ShapeDtypeStruct((b, s, LANES), F32),) * 2,
        grid=(b, s // ts),
        in_specs=[pl.BlockSpec((1, 1, ts), lambda i, j: (i, 0, j)),
                  pl.BlockSpec((ROT_DIM, 1), lambda i, j: (0, 0)),
                  pl.BlockSpec((ROT_DIM, 1), lambda i, j: (0, 0))],
        out_specs=(pl.BlockSpec((1, ts, LANES), lambda i, j: (i, j, 0)),) * 2,
        compiler_params=_params(),
        name="rope_tables",
    )(positions.reshape(b, 1, s), freq, sign)


def _ffn_kernel(x_ref, g_ref, wg_ref, wu_ref, wo_ref, o_ref):
    x = x_ref[...]
    h = _rms(x, g_ref[...]).astype(BF16)
    d_ff = wg_ref.shape[1]
    acc = jnp.zeros_like(x)
    for c in range(d_ff // FFN_CHUNK):
        sl = slice(c * FFN_CHUNK, (c + 1) * FFN_CHUNK)
        g = jnp.dot(h, wg_ref[:, sl], preferred_element_type=F32)
        u = jnp.dot(h, wu_ref[:, sl], preferred_element_type=F32)
        a = (g * jax.nn.sigmoid(g) * u).astype(BF16)
        acc = acc + jnp.dot(a, wo_ref[sl, :], preferred_element_type=F32)
    o_ref[...] = x + 0.5 * acc


def _ffn(x2, gain, w_in, w_out):
    t, d = x2.shape
    d_ff = w_out.shape[0]
    wg = w_in[:, :d_ff].astype(BF16)
    wu = w_in[:, d_ff:].astype(BF16)
    wo = w_out.astype(BF16)
    tm = min(TM_FFN, t)
    return pl.pallas_call(
        _ffn_kernel,
        out_shape=jax.ShapeDtypeStruct((t, d), F32),
        grid=(t // tm,),
        in_specs=[pl.BlockSpec((tm, d), lambda i: (i, 0)),
                  _resident((1, d)), _resident((d, d_ff)), _resident((d, d_ff)), _resident((d_ff, d))],
        out_specs=pl.BlockSpec((tm, d), lambda i: (i, 0)),
        compiler_params=_params(),
        name="ffn_half_step",
    )(x2, gain[None, :], wg, wu, wo)


def _memkv_kernel(mem_ref, gn_ref, w_ref, kg_ref, k_ref, v_ref):
    width = k_ref.shape[2]
    mn = _rms(mem_ref[0], gn_ref[...]).astype(BF16)
    kv = jnp.dot(mn, w_ref[...], preferred_element_type=F32)
    for h in range(MEM_HEADS):
        sl = slice(h * MEM_HEAD_DIM, (h + 1) * MEM_HEAD_DIM)
        k_ref[0, :, sl] = _rms(kv[:, sl], kg_ref[...]).astype(BF16)
    v_ref[0] = kv[:, width:].astype(BF16)


def _mem_kv(mem, norm_gain, w_kv, k_gain):
    b, m, d = mem.shape
    width = MEM_HEADS * MEM_HEAD_DIM
    return pl.pallas_call(
        _memkv_kernel,
        out_shape=(jax.ShapeDtypeStruct((b, m, width), BF16),) * 2,
        grid=(b,),
        in_specs=[pl.BlockSpec((1, m, d), lambda i: (i, 0, 0)),
                  _resident((1, d)), _resident((d, 2 * width)), _resident((1, MEM_HEAD_DIM))],
        out_specs=(pl.BlockSpec((1, m, width), lambda i: (i, 0, 0)),) * 2,
        compiler_params=_params(),
        name="memory_kv",
    )(mem, norm_gain[None, :], w_kv.astype(BF16), k_gain[None, :])


def _proj_kernel(x_ref, gn_ref, wqkv_ref, wf_ref, wp_ref, wmq_ref, bf_ref, gains_ref,
                 cos_ref, sin_ref, pmix_ref, pscale_ref, km_ref, vm_ref, mqg_ref,
                 qa_ref, ka_ref, va_ref, qc_ref, kc_ref, vc_ref, c_ref, ob_ref, om_ref, kmean_ref,
                 pool_carry, c_carry):
    j = pl.program_id(1)
    tm = x_ref.shape[1]
    width = qa_ref.shape[2]
    n_grp = width // LANES

    @pl.when(j == 0)
    def _():
        pool_carry[...] = jnp.zeros_like(pool_carry)
        c_carry[...] = jnp.zeros_like(c_carry)

    h = _rms(x_ref[0], gn_ref[...]).astype(BF16)
    scale = HEAD_DIM ** -0.5 * LOG2E
    cos = cos_ref[0]
    sin = sin_ref[0]
    kmean_ref[...] = jnp.zeros_like(kmean_ref)

    def project(w_ref, lo=None, hi=None):
        return lambda: jnp.dot(h, w_ref[...] if lo is None else w_ref[:, lo:hi], preferred_element_type=F32)

    def finish_heads(out_ref, gain_row=None, rot=False, mul=None, means=False):
        def run(y):
            for g in range(n_grp):
                sl = slice(g * LANES, (g + 1) * LANES)
                t = y[:, sl]
                if gain_row is not None:
                    t = _head_pair_rms(t, gains_ref[gain_row:gain_row + 1, :])
                if rot:
                    t = _rotate(t, cos, sin)
                if means:
                    for sb in range(tm // MOBA_BLOCK):
                        kmean_ref[0, 0, sb:sb + 1, sl] = jnp.mean(
                            t[sb * MOBA_BLOCK:(sb + 1) * MOBA_BLOCK], axis=0, keepdims=True)
                if mul is not None:
                    t = t * mul
                out_ref[0, :, sl] = t.astype(BF16)
        return run

    def finish_forget(y):
        lf = jax.nn.log_sigmoid(y + bf_ref[...])
        row = lax.broadcasted_iota(jnp.int32, lf.shape, 0)
        step = 1
        while step < tm:
            lf = lf + jnp.where(row >= step, pltpu.roll(lf, step, axis=0), 0.0)
            step *= 2
        c = lf + c_carry[...]
        c_ref[0] = c
        c_carry[...] = c[tm - 1:tm, :]

    def finish_pool(u):
        ext = jnp.concatenate([pool_carry[...], u], axis=0)
        pos = j * tm + lax.broadcasted_iota(jnp.int32, (tm, 1), 0)
        pooled = []
        for g, w in enumerate(POOL_WINDOWS):
            sl = slice(g * POOL_GROUP, (g + 1) * POOL_GROUP)
            win = ext[:, sl]
            step = 1
            while step < w:
                win = win + pltpu.roll(win, step, axis=0)
                step *= 2
            cnt = jnp.minimum(pos + 1, w).astype(F32)
            pooled.append((win[POOL_HALO:, :] / cnt - u[:, sl]).astype(BF16))
        for g in range(len(POOL_WINDOWS)):
            sl = slice(g * POOL_GROUP, (g + 1) * POOL_GROUP)
            mixed = jnp.dot(pooled[g], pmix_ref[g], preferred_element_type=F32)
            ob_ref[0, :, sl] = (mixed * pscale_ref[:, sl]).astype(BF16)
        pool_carry[...] = u[tm - POOL_HALO:, :]

    def finish_memory(qm_all):
        heads = [slice(hd * MEM_HEAD_DIM, (hd + 1) * MEM_HEAD_DIM) for hd in range(MEM_HEADS)]
        scores = [lax.dot_general(_rms(qm_all[:, sl], mqg_ref[...]).astype(BF16), km_ref[0, :, sl], _NT,
                                  preferred_element_type=F32) * (MEM_HEAD_DIM ** -0.5) for sl in heads]
        probs = [jnp.exp(s - jnp.max(s, axis=-1, keepdims=True)) for s in scores]
        for sl, p in zip(heads, probs):
            den = jnp.sum(p, axis=-1, keepdims=True)
            o = jnp.dot(p.astype(BF16), vm_ref[0, :, sl], preferred_element_type=F32)
            om_ref[0, :, sl] = (o / den).astype(BF16)

    stages = [
        (project(wqkv_ref, 0 * width, 1 * width), finish_heads(qa_ref, 0, rot=True, mul=scale)),
        (project(wqkv_ref, 1 * width, 2 * width), finish_heads(ka_ref, 1, rot=True, means=True)),
        (project(wqkv_ref, 2 * width, 3 * width), finish_heads(va_ref)),
        (project(wqkv_ref, 3 * width, 4 * width), finish_heads(qc_ref, 2, mul=scale)),
        (project(wqkv_ref, 4 * width, 5 * width), finish_heads(kc_ref, 3)),
        (project(wqkv_ref, 5 * width, 6 * width), finish_heads(vc_ref)),
        (project(wf_ref), finish_forget),
        (project(wp_ref), finish_pool),
        (project(wmq_ref), finish_memory),
    ]
    pending = None
    for matmul, finish in stages:
        y = matmul()
        if pending is not None:
            pending[0](pending[1])
        pending = (finish, y)
    pending[0](pending[1])


def _project(x3, norm_gain, w_in, b_forget, moba_q_gain, moba_k_gain, fox_q_gain, fox_k_gain,
             cos, sin, pool_mix, pool_scale, k_m, v_m, mem_q_gain):
    b, s, d = x3.shape
    width = MOBA_HEADS * HEAD_DIM
    pool_width = POOL_GROUP * len(POOL_WINDOWS)
    mem_width = MEM_HEADS * MEM_HEAD_DIM
    m_len = k_m.shape[1]
    tm = min(TM_PROJ, s)
    nt = s // tm
    o0 = 6 * width
    w_qkv = w_in[:, :o0].astype(BF16)
    w_f = jnp.pad(w_in[:, o0:o0 + FOX_HEADS], ((0, 0), (0, LANES - FOX_HEADS))).astype(BF16)
    b_f = jnp.pad(b_forget, (0, LANES - FOX_HEADS))[None, :]
    o1 = o0 + FOX_HEADS
    w_pool = w_in[:, o1:o1 + pool_width].astype(BF16)
    o2 = o1 + pool_width
    w_mq = w_in[:, o2:o2 + mem_width].astype(BF16)
    gains = jnp.stack([jnp.tile(g, HEADS_PER_GROUP)
                       for g in (moba_q_gain, moba_k_gain, fox_q_gain, fox_k_gain)])

    tile = lambda w: pl.BlockSpec((1, tm, w), lambda i, j: (i, j, 0))
    act = lambda w: jax.ShapeDtypeStruct((b, s, w), BF16)
    outs = pl.pallas_call(
        _proj_kernel,
        out_shape=(act(width),) * 6 + (jax.ShapeDtypeStruct((b, s, LANES), F32), act(pool_width),
                                       act(mem_width), jax.ShapeDtypeStruct((b, nt, 8, width), F32)),
        grid=(b, nt),
        in_specs=[tile(d), _resident((1, d)), _resident((d, o0)), _resident((d, LANES)),
                  _resident((d, pool_width)), _resident((d, mem_width)), _resident((1, LANES)),
                  _resident((4, LANES)), tile(LANES), tile(LANES),
                  _resident((len(POOL_WINDOWS), POOL_GROUP, POOL_GROUP)), _resident((1, pool_width)),
                  pl.BlockSpec((1, m_len, mem_width), lambda i, j: (i, 0, 0)),
                  pl.BlockSpec((1, m_len, mem_width), lambda i, j: (i, 0, 0)),
                  _resident((1, MEM_HEAD_DIM))],
        out_specs=(tile(width),) * 6 + (tile(LANES), tile(pool_width), tile(mem_width),
                                        pl.BlockSpec((1, 1, 8, width), lambda i, j: (i, j, 0, 0))),
        scratch_shapes=[pltpu.VMEM((POOL_HALO, pool_width), F32), pltpu.VMEM((1, LANES), F32)],
        compiler_params=pltpu.CompilerParams(dimension_semantics=("arbitrary", "arbitrary"),
                                             vmem_limit_bytes=VMEM_LIMIT),
        name="mixer_projection",
    )(x3, norm_gain[None, :], w_qkv, w_f, w_pool, w_mq, b_f, gains, cos, sin,
      pool_mix.astype(BF16), pool_scale[None, :], k_m, v_m, mem_q_gain[None, :])
    qa, ka, va, qc, kc, vc, c, ob, om, kmean = outs
    kmean = kmean[:, :, :tm // MOBA_BLOCK, :].reshape(b, s // MOBA_BLOCK, width)
    return qa, ka, va, qc, kc, vc, c, ob, om, kmean


def _stage_values(v_ref, vt_ref):
    vt = v_ref[0].T
    s_len = vt.shape[1]
    row = lax.broadcasted_iota(jnp.int32, (AUG_ROWS, s_len), 0)
    tail = jnp.where(row == 0, 1.0, 0.0).astype(BF16)
    for hh in range(HEADS_PER_GROUP):
        vt_ref[hh] = jnp.concatenate([vt[hh * HEAD_DIM:(hh + 1) * HEAD_DIM], tail], axis=0)


def _stage_keys(k_ref, kaug_ref, r0, rows, ext0, ext1):
    k = k_ref[0, pl.ds(r0, rows), :]
    lane = lax.broadcasted_iota(jnp.int32, k.shape, 1)
    kaug_ref[0, pl.ds(r0, rows), :] = jnp.where(lane < HEAD_DIM, k, ext0.astype(BF16))
    kaug_ref[1, pl.ds(r0, rows), :] = jnp.where(lane >= HEAD_DIM, k, ext1.astype(BF16))


def _augment_queries(qt, e0, e1):
    tq = qt.shape[1]
    pad = jnp.zeros((HEAD_DIM - AUG_ROWS, tq), F32)
    qa0 = jnp.concatenate([qt[:HEAD_DIM], e0, pad], axis=0).astype(BF16)
    qa1 = jnp.concatenate([e1, pad, qt[HEAD_DIM:]], axis=0).astype(BF16)
    return qa0, qa1


def _issue_scores(qa, kaug_ref, s_ref, jb, slot):
    k0 = pl.multiple_of(jb * TK_ATTN, TK_ATTN)
    for hh in range(HEADS_PER_GROUP):
        s_ref[slot, hh] = jnp.dot(kaug_ref[hh, pl.ds(k0, TK_ATTN), :], qa[hh], preferred_element_type=F32)


def _attend_pair(qa, next_queries, kaug_ref, vt_ref, s_ref, i, o_ref):
    tq = o_ref.shape[0]
    tk = TK_ATTN
    key_i = lax.broadcasted_iota(jnp.int32, (tk, tq), 0)
    qry_i = lax.broadcasted_iota(jnp.int32, (tk, tq), 1)

    def issue_scores(jb, slot):
        _issue_scores(qa, kaug_ref, s_ref, jb, slot)

    def block_max(slot, diag=None):
        out = []
        for hh in range(HEADS_PER_GROUP):
            s = s_ref[slot, hh]
            if diag is not None:
                s = jnp.where(key_i + diag * tk <= qry_i, s, NEG_INF)
            out.append(jnp.max(s, axis=0, keepdims=True))
        return tuple(out)

    def accumulate(jb, slot, carry, mx, diag=None):
        k0 = pl.multiple_of(jb * tk, tk)
        new = []
        for hh in range(HEADS_PER_GROUP):
            m, acc = carry[hh]
            s = s_ref[slot, hh]
            if diag is not None:
                s = jnp.where(key_i + diag * tk <= qry_i, s, NEG_INF)
            m_new = jnp.maximum(m, mx[hh])
            alpha = jnp.exp2(m - m_new)
            p = jnp.exp2(s - m_new).astype(BF16)
            vb = vt_ref[hh, :, pl.ds(k0, tk)]
            new.append((m_new, alpha * acc + jnp.dot(vb, p, preferred_element_type=F32)))
        return tuple(new)

    def step(t, state):
        carry, mx = state
        issue_scores(2 * t + 1, 1)
        carry = accumulate(2 * t, 0, carry, mx)
        mx = block_max(1)
        issue_scores(2 * t + 2, 0)
        carry = accumulate(2 * t + 1, 1, carry, mx)
        return carry, block_max(0)

    init = tuple((jnp.full((1, tq), NEG_INF, F32), jnp.zeros((V_ROWS, tq), F32))
                 for _ in range(HEADS_PER_GROUP))
    state = lax.fori_loop(0, i // 2, lambda u, st: step(2 * u + 1, step(2 * u, st)), (init, block_max(0)))
    carry, _ = lax.cond(i % 2 == 1, lambda st: step(i - 1, st), lambda st: st, state)
    half = tq // 2
    k_last = pl.multiple_of((2 * i + 1) * tk, tk)
    for hh in range(HEADS_PER_GROUP):
        s_ref[1, hh, :, half:] = jnp.dot(kaug_ref[hh, pl.ds(k_last, tk), :], qa[hh][:, half:],
                                         preferred_element_type=F32)
    qa_next = next_queries()
    carry = accumulate(2 * i, 0, carry, block_max(0, diag=0), diag=0)
    _issue_scores(qa_next, kaug_ref, s_ref, 0, 0)
    causal_half = (lax.broadcasted_iota(jnp.int32, (tk, half), 0)
                   <= lax.broadcasted_iota(jnp.int32, (tk, half), 1))
    accs = []
    for hh in range(HEADS_PER_GROUP):
        m, acc = carry[hh]
        s = jnp.where(causal_half, s_ref[1, hh, :, half:], NEG_INF)
        m_new = jnp.maximum(m[:, half:], jnp.max(s, axis=0, keepdims=True))
        alpha = jnp.exp2(m[:, half:] - m_new)
        p = jnp.exp2(s - m_new).astype(BF16)
        right = alpha * acc[:, half:] + jnp.dot(vt_ref[hh, :, pl.ds(k_last, tk)], p, preferred_element_type=F32)
        accs.append(jnp.concatenate([acc[:, :half], right], axis=1))
    ot = jnp.concatenate([acc[:HEAD_DIM] / acc[HEAD_DIM:HEAD_DIM + 1] for acc in accs], axis=0)
    o_ref[...] = ot.T.astype(o_ref.dtype)
    return qa_next


def _split3(x):
    hi = x.astype(BF16).astype(F32)
    r = x - hi
    mid = r.astype(BF16).astype(F32)
    return hi, mid, r - mid


def _fox_kernel(q_ref, k_ref, v_ref, c_ref, crow_ref, o_ref, kaug_ref, vt_ref, s_ref):
    g = pl.program_id(1)
    tq = TQ_ATTN
    s_len = k_ref.shape[1]
    n_parts = 3

    _stage_values(v_ref, vt_ref)
    r = lax.broadcasted_iota(jnp.int32, (n_parts * LANES, 2 * LANES), 0)
    col = lax.broadcasted_iota(jnp.int32, (n_parts * LANES, 2 * LANES), 1)
    hit = jnp.zeros(r.shape, jnp.bool_)
    for part in range(n_parts):
        hit = hit | (((r == part * LANES + HEADS_PER_GROUP * g) & (col == HEAD_DIM + n_parts + part))
                     | ((r == part * LANES + HEADS_PER_GROUP * g + 1) & (col == LANES + n_parts + part)))
    place = jnp.where(hit, -1.0, 0.0).astype(BF16)
    lane2 = lax.broadcasted_iota(jnp.int32, (1, 2 * LANES), 1)
    ones = jnp.where(((lane2 >= HEAD_DIM) & (lane2 < HEAD_DIM + n_parts))
                     | ((lane2 >= LANES) & (lane2 < LANES + n_parts)), 1.0, 0.0)
    rows = min(512, s_len)

    def chunk(ci, _):
        r0 = pl.multiple_of(ci * rows, rows)
        parts = jnp.concatenate([p.astype(BF16) for p in _split3(c_ref[0, pl.ds(r0, rows), :] * LOG2E)], axis=1)
        ext = ones + jnp.dot(parts, place, preferred_element_type=F32)
        _stage_keys(k_ref, kaug_ref, r0, rows, ext[:, :LANES], ext[:, LANES:])
        return 0

    lax.fori_loop(0, s_len // rows, chunk, 0)

    row = lax.broadcasted_iota(jnp.int32, (AUG_ROWS, tq), 0)

    def queries(i):
        q0 = pl.multiple_of(i * tq, tq)
        qt = q_ref[0, pl.ds(q0, tq), :].astype(F32).T
        exts = []
        for hh in range(HEADS_PER_GROUP):
            hi, mid, lo = _split3(crow_ref[0, 0, hh:hh + 1, pl.ds(q0, tq)] * LOG2E)
            e = jnp.where(row == 0, hi, jnp.where(row == 1, mid, jnp.where(row == 2, lo, 0.0)))
            exts.append(jnp.where((row >= n_parts) & (row < 2 * n_parts), 1.0, e))
        return _augment_queries(qt, *exts)

    _run_query_tiles(queries, kaug_ref, vt_ref, s_ref, o_ref, s_len // tq)


def _run_query_tiles(queries, kaug_ref, vt_ref, s_ref, o_ref, n_tiles):
    tq = TQ_ATTN

    def tile(i, qa):
        q0 = pl.multiple_of(i * tq, tq)
        nxt = lambda: queries(jnp.minimum(i + 1, n_tiles - 1))
        return _attend_pair(qa, nxt, kaug_ref, vt_ref, s_ref, i, o_ref.at[0, pl.ds(q0, tq), :])

    first = queries(0)
    _issue_scores(first, kaug_ref, s_ref, 0, 0)
    lax.fori_loop(0, n_tiles, tile, first)


def _attention_call(kernel_fn, name, q, k, v, extra_inputs, extra_specs):
    b, s, width = q.shape
    n_grp = width // LANES
    tq = TQ_ATTN
    assert tq == 2 * TK_ATTN and s % tq == 0
    seq = pl.BlockSpec((1, s, LANES), lambda bi, g: (bi, 0, g))
    return pl.pallas_call(
        kernel_fn,
        out_shape=jax.ShapeDtypeStruct((b, s, width), BF16),
        grid=(b, n_grp),
        in_specs=[seq, seq, seq] + extra_specs,
        out_specs=seq,
        scratch_shapes=[pltpu.VMEM((HEADS_PER_GROUP, s, LANES), BF16),
                        pltpu.VMEM((HEADS_PER_GROUP, V_ROWS, s), BF16),
                        pltpu.VMEM((2, HEADS_PER_GROUP, TK_ATTN, tq), F32)],
        compiler_params=_params(),
        name=name,
    )(q, k, v, *extra_inputs)


def _fox_attention(q, k, v, c):
    b, s, width = q.shape
    n_grp = width // LANES
    crow = c[:, :, :FOX_HEADS].transpose(0, 2, 1).reshape(b, n_grp, HEADS_PER_GROUP, s)
    return _attention_call(
        _fox_kernel, "forgetting_attention", q, k, v, (c, crow),
        [pl.BlockSpec((1, s, LANES), lambda bi, g: (bi, 0, 0)),
         pl.BlockSpec((1, 1, HEADS_PER_GROUP, s), lambda bi, g: (bi, g, 0, 0))])


def _moba_kernel(q_ref, k_ref, v_ref, kmean_ref, o_ref, kaug_ref, vt_ref, s_ref):
    tq = TQ_ATTN
    s_len = k_ref.shape[1]

    _stage_values(v_ref, vt_ref)
    lane_k = lax.broadcasted_iota(jnp.int32, (MOBA_BLOCK, LANES), 1)
    lane_1 = lax.broadcasted_iota(jnp.int32, (1, LANES), 1)

    def stage_block(n, _):
        r0 = pl.multiple_of(n * MOBA_BLOCK, MOBA_BLOCK)
        k = k_ref[0, pl.ds(r0, MOBA_BLOCK), :]
        hot_hi = jnp.where(lane_1 == n + HEAD_DIM, 1.0, 0.0).astype(BF16)
        hot_lo = jnp.where(lane_1 == n, 1.0, 0.0).astype(BF16)
        kaug_ref[0, pl.ds(r0, MOBA_BLOCK), :] = jnp.where(lane_k < HEAD_DIM, k, hot_hi)
        kaug_ref[1, pl.ds(r0, MOBA_BLOCK), :] = jnp.where(lane_k >= HEAD_DIM, k, hot_lo)
        return 0

    lax.fori_loop(0, s_len // MOBA_BLOCK, stage_block, 0)

    kmean = kmean_ref[0]
    km_hi = kmean.astype(BF16)
    km_lo = (kmean - km_hi.astype(F32)).astype(BF16)
    zeros = jnp.zeros((HEAD_DIM, tq), F32)
    row = lax.broadcasted_iota(jnp.int32, (AUG_ROWS, tq), 0)
    row_f = row.astype(F32)
    qpos = lax.broadcasted_iota(jnp.int32, (AUG_ROWS, tq), 1)

    def queries(i):
        q0 = pl.multiple_of(i * tq, tq)
        qt = q_ref[0, pl.ds(q0, tq), :].astype(F32).T
        own = (q0 + qpos) // MOBA_BLOCK
        past = row < own
        exts = []
        for hh in range(HEADS_PER_GROUP):
            qz = (jnp.concatenate([qt[:HEAD_DIM], zeros], axis=0) if hh == 0
                  else jnp.concatenate([zeros, qt[HEAD_DIM:]], axis=0)).astype(BF16)
            gate = (jnp.dot(km_hi, qz, preferred_element_type=F32)
                    + jnp.dot(km_lo, qz, preferred_element_type=F32))
            gate = jnp.where(past, gate, NEG_INF)
            keep = row == own
            for _ in range(MOBA_TOPK):
                top = jnp.max(gate, axis=0, keepdims=True)
                idx = jnp.min(jnp.where(gate == top, row_f, float(AUG_ROWS)), axis=0, keepdims=True)
                pick = row_f == idx
                keep = keep | (pick & past)
                gate = jnp.where(pick, -jnp.inf, gate)
            exts.append(jnp.where(keep, 0.0, NEG_INF))
        return _augment_queries(qt, *exts)

    _run_query_tiles(queries, kaug_ref, vt_ref, s_ref, o_ref, s_len // tq)


def _moba_attention(q, k, v, kmean):
    b, s, width = q.shape
    n_blk = s // MOBA_BLOCK
    assert n_blk <= AUG_ROWS and TK_ATTN == MOBA_BLOCK
    kmean = jnp.pad(kmean, ((0, 0), (0, AUG_ROWS - n_blk), (0, 0)))
    return _attention_call(
        _moba_kernel, "moba_attention", q, k, v, (kmean,),
        [pl.BlockSpec((1, AUG_ROWS, LANES), lambda bi, g: (bi, 0, g))])


def _merge_kernel(x_ref, gn_ref, oa_ref, ob_ref, oc_ref, om_ref, wg_ref, wbr_ref, wo_ref, o_ref):
    x = x_ref[...]
    d = x.shape[1]
    h = _rms(x, gn_ref[...]).astype(BF16)
    mix = jnp.zeros_like(x)
    for n, br in enumerate((oa_ref, ob_ref, oc_ref, om_ref)):
        gate = jax.nn.sigmoid(jnp.dot(h, wg_ref[:, n * d:(n + 1) * d], preferred_element_type=F32))
        y = jnp.dot(br[...], wbr_ref[n], preferred_element_type=F32)
        mix = mix + gate * y
    o_ref[...] = x + jnp.dot(mix.astype(BF16), wo_ref[...], preferred_element_type=F32)


def _merge(x2, norm_gain, branches, w_gate, w_branch, w_out):
    t, d = x2.shape
    bw = w_branch.shape[1]
    tm = min(TM_FFN, t)
    row = lambda w: pl.BlockSpec((tm, w), lambda i: (i, 0))
    return pl.pallas_call(
        _merge_kernel,
        out_shape=jax.ShapeDtypeStruct((t, d), F32),
        grid=(t // tm,),
        in_specs=[row(d), _resident((1, d)), row(bw), row(bw), row(bw), row(bw),
                  _resident((d, N_BRANCH * d)), _resident((N_BRANCH, bw, d)), _resident((d, d))],
        out_specs=row(d),
        compiler_params=_params(),
        name="gated_merge",
    )(x2, norm_gain[None, :], *branches, w_gate.astype(BF16), w_branch.astype(BF16), w_out.astype(BF16))


def kernel(x, mem, positions, norm_ffn1, w_ffn1_in, w_ffn1_out, norm_mix, w_in, b_forget, moba_q_gain, moba_k_gain, fox_q_gain, fox_k_gain, norm_mem, w_mem_kv, mem_q_gain, mem_k_gain, pool_mix, pool_scale, w_branch, w_out, norm_ffn2, w_ffn2_in, w_ffn2_out):
    b, s, d = x.shape
    depth = norm_ffn1.shape[0]
    gate_off = w_in.shape[2] - N_BRANCH * d
    cos, sin = _rope_tables(positions)
    x2 = x.reshape(b * s, d)
    for l in range(depth):
        x2 = _ffn(x2, norm_ffn1[l], w_ffn1_in[l], w_ffn1_out[l])
        k_m, v_m = _mem_kv(mem, norm_mem[l], w_mem_kv[l], mem_k_gain[l])
        qa, ka, va, qc, kc, vc, c, ob, om, kmean = _project(
            x2.reshape(b, s, d), norm_mix[l], w_in[l, :, :gate_off], b_forget[l],
            moba_q_gain[l], moba_k_gain[l], fox_q_gain[l], fox_k_gain[l],
            cos, sin, pool_mix[l], pool_scale[l], k_m, v_m, mem_q_gain[l])
        oa = _moba_attention(qa, ka, va, kmean)
        oc = _fox_attention(qc, kc, vc, c)
        flat = lambda a: a.reshape(b * s, a.shape[2])
        x2 = _merge(x2, norm_mix[l], (flat(oa), flat(ob), flat(oc), flat(om)),
                    w_in[l, :, gate_off:], w_branch[l], w_out[l])
        x2 = _ffn(x2, norm_ffn2[l], w_ffn2_in[l], w_ffn2_out[l])
    return x2.reshape(b, s, d)
```

```python
import jax
import jax.numpy as jnp
from jax import lax
from jax.experimental import pallas as pl
from jax.experimental.pallas import tpu as pltpu

HEAD_DIM = 64
MOBA_HEADS = 8
MOBA_BLOCK = 256
MOBA_TOPK = 3
FOX_HEADS = 8
POOL_WINDOWS = (2, 4, 8, 16)
POOL_GROUP = 128
MEM_HEADS = 4
MEM_HEAD_DIM = 128
ROPE_THETA = 500000.0
ROT_DIM = HEAD_DIM // 4
N_BRANCH = 4
RMS_EPS = 1e-6
NEG_INF = -1e30

LANES = 128
HEADS_PER_GROUP = LANES // HEAD_DIM
POOL_HALO = 16
VMEM_LIMIT = 56 * 1024 * 1024

TM_FFN = 1024
TM_PROJ = 1024
TQ_ATTN = 512
TK_ATTN = 256
STAGE_ROWS = 512
AUG_ROWS = 16
FFN_CHUNK = 256

LOG2E = 1.4426950408889634
V_ROWS = HEAD_DIM + AUG_ROWS

F32 = jnp.float32
BF16 = jnp.bfloat16
_NT = (((1,), (1,)), ((), ()))


def _params():
    return pltpu.CompilerParams(vmem_limit_bytes=VMEM_LIMIT)


def _resident(shape):
    nd = len(shape)
    return pl.BlockSpec(shape, lambda *_: (0,) * nd, pipeline_mode=pl.Buffered(1))


def _rms(x, gain):
    ms = jnp.mean(x * x, axis=-1, keepdims=True)
    return x * lax.rsqrt(ms + RMS_EPS) * gain


def _head_pair_rms(x, gain):
    lane = lax.broadcasted_iota(jnp.int32, x.shape, 1)
    lo = lane < HEAD_DIM
    sq = x * x
    s_lo = jnp.sum(jnp.where(lo, sq, 0.0), axis=-1, keepdims=True)
    s_hi = jnp.sum(jnp.where(lo, 0.0, sq), axis=-1, keepdims=True)
    ms = jnp.where(lo, s_lo, s_hi) * (1.0 / HEAD_DIM)
    return x * lax.rsqrt(ms + RMS_EPS) * gain


def _rotate(x, cos, sin):
    lane = lax.broadcasted_iota(jnp.int32, x.shape, 1)
    half = ROT_DIM // 2
    fwd = pltpu.roll(x, LANES - half, axis=1)
    bwd = pltpu.roll(x, half, axis=1)
    partner = jnp.where((lane % HEAD_DIM) < half, fwd, bwd)
    return x * cos + partner * sin


def _rope_kernel(pos_ref, freq_ref, sign_ref, cos_ref, sin_ref):
    ang = freq_ref[...] * pos_ref[0].astype(F32)
    cos_r = jnp.cos(ang)
    sin_r = jnp.sin(ang) * sign_ref[...]
    ts = ang.shape[1]
    one = jnp.ones((HEAD_DIM - ROT_DIM, ts), F32)
    zero = jnp.zeros((HEAD_DIM - ROT_DIM, ts), F32)
    cos_ref[0] = jnp.concatenate([cos_r, one] * HEADS_PER_GROUP, axis=0).T
    sin_ref[0] = jnp.concatenate([sin_r, zero] * HEADS_PER_GROUP, axis=0).T


def _rope_tables(positions):
    b, s = positions.shape
    half = ROT_DIM // 2
    inv_freq = jnp.power(ROPE_THETA, -jnp.arange(half, dtype=F32) * (2.0 / ROT_DIM))
    freq = jnp.tile(inv_freq, 2)[:, None]
    sign = jnp.concatenate([-jnp.ones((half,), F32), jnp.ones((half,), F32)])[:, None]
    ts = min(s, 1024)
    return pl.pallas_call(
        _rope_kernel,
        out_shape=(jax.ShapeDtypeStruct((b, s, LANES), F32),) * 2,
        grid=(b, s // ts),
        in_specs=[pl.BlockSpec((1, 1, ts), lambda i, j: (i, 0, j)),
                  pl.BlockSpec((ROT_DIM, 1), lambda i, j: (0, 0)),
                  pl.BlockSpec((ROT_DIM, 1), lambda i, j: (0, 0))],
        out_specs=(pl.BlockSpec((1, ts, LANES), lambda i, j: (i, j, 0)),) * 2,
        compiler_params=_params(),
        name="rope_tables",
    )(positions.reshape(b, 1, s), freq, sign)


def _ffn_kernel(x_ref, g_ref, wg_ref, wu_ref, wo_ref, o_ref):
    x = x_ref[...]
    h = _rms(x, g_ref[...]).astype(BF16)
    d_ff = wg_ref.shape[1]
    acc = jnp.zeros_like(x)
    for c in range(d_ff // FFN_CHUNK):
        sl = slice(c * FFN_CHUNK, (c + 1) * FFN_CHUNK)
        g = jnp.dot(h, wg_ref[:, sl], preferred_element_type=F32)
        u = jnp.dot(h, wu_ref[:, sl], preferred_element_type=F32)
        a = (g * jax.nn.sigmoid(g) * u).astype(BF16)
        acc = acc + jnp.dot(a, wo_ref[sl, :], preferred_element_type=F32)
    o_ref[...] = x + 0.5 * acc


def _ffn(x2, gain, w_in, w_out):
    t, d = x2.shape
    d_ff = w_out.shape[0]
    wg = w_in[:, :d_ff].astype(BF16)
    wu = w_in[:, d_ff:].astype(BF16)
    wo = w_out.astype(BF16)
    tm = min(TM_FFN, t)
    return pl.pallas_call(
        _ffn_kernel,
        out_shape=jax.ShapeDtypeStruct((t, d), F32),
        grid=(t // tm,),
        in_specs=[pl.BlockSpec((tm, d), lambda i: (i, 0)),
                  _resident((1, d)), _resident((d, d_ff)), _resident((d, d_ff)), _resident((d_ff, d))],
        out_specs=pl.BlockSpec((tm, d), lambda i: (i, 0)),
        compiler_params=_params(),
        name="ffn_half_step",
    )(x2, gain[None, :], wg, wu, wo)


def _memkv_kernel(mem_ref, gn_ref, w_ref, kg_ref, k_ref, v_ref):
    width = k_ref.shape[2]
    mn = _rms(mem_ref[0], gn_ref[...]).astype(BF16)
    kv = jnp.dot(mn, w_ref[...], preferred_element_type=F32)
    for h in range(MEM_HEADS):
        sl = slice(h * MEM_HEAD_DIM, (h + 1) * MEM_HEAD_DIM)
        k_ref[0, :, sl] = _rms(kv[:, sl], kg_ref[...]).astype(BF16)
    v_ref[0] = kv[:, width:].astype(BF16)


def _mem_kv(mem, norm_gain, w_kv, k_gain):
    b, m, d = mem.shape
    width = MEM_HEADS * MEM_HEAD_DIM
    return pl.pallas_call(
        _memkv_kernel,
        out_shape=(jax.ShapeDtypeStruct((b, m, width), BF16),) * 2,
        grid=(b,),
        in_specs=[pl.BlockSpec((1, m, d), lambda i: (i, 0, 0)),
                  _resident((1, d)), _resident((d, 2 * width)), _resident((1, MEM_HEAD_DIM))],
        out_specs=(pl.BlockSpec((1, m, width), lambda i: (i, 0, 0)),) * 2,
        compiler_params=_params(),
        name="memory_kv",
    )(mem, norm_gain[None, :], w_kv.astype(BF16), k_gain[None, :])


def _proj_kernel(x_ref, gn_ref, wqkv_ref, wf_ref, wp_ref, wmq_ref, bf_ref, gains_ref,
                 cos_ref, sin_ref, pmix_ref, pscale_ref, km_ref, vm_ref, mqg_ref,
                 qa_ref, ka_ref, va_ref, qc_ref, kc_ref, vc_ref, c_ref, ob_ref, om_ref, kmean_ref,
                 pool_carry, c_carry):
    j = pl.program_id(1)
    tm = x_ref.shape[1]
    width = qa_ref.shape[2]
    n_grp = width // LANES

    @pl.when(j == 0)
    def _():
        pool_carry[...] = jnp.zeros_like(pool_carry)
        c_carry[...] = jnp.zeros_like(c_carry)

    h = _rms(x_ref[0], gn_ref[...]).astype(BF16)
    scale = HEAD_DIM ** -0.5 * LOG2E
    cos = cos_ref[0]
    sin = sin_ref[0]
    kmean_ref[...] = jnp.zeros_like(kmean_ref)

    def project(w_ref, lo=None, hi=None):
        return lambda: jnp.dot(h, w_ref[...] if lo is None else w_ref[:, lo:hi], preferred_element_type=F32)

    def finish_heads(out_ref, gain_row=None, rot=False, mul=None, means=False):
        def run(y):
            for g in range(n_grp):
                sl = slice(g * LANES, (g + 1) * LANES)
                t = y[:, sl]
                if gain_row is not None:
                    t = _head_pair_rms(t, gains_ref[gain_row:gain_row + 1, :])
                if rot:
                    t = _rotate(t, cos, sin)
                if means:
                    for sb in range(tm // MOBA_BLOCK):
                        kmean_ref[0, 0, sb:sb + 1, sl] = jnp.mean(
                            t[sb * MOBA_BLOCK:(sb + 1) * MOBA_BLOCK], axis=0, keepdims=True)
                if mul is not None:
                    t = t * mul
                out_ref[0, :, sl] = t.astype(BF16)
        return run

    def finish_forget(y):
        lf = jax.nn.log_sigmoid(y + bf_ref[...])
        row = lax.broadcasted_iota(jnp.int32, lf.shape, 0)
        step = 1
        while step < tm:
            lf = lf + jnp.where(row >= step, pltpu.roll(lf, step, axis=0), 0.0)
            step *= 2
        c = lf + c_carry[...]
        c_ref[0] = c
        c_carry[...] = c[tm - 1:tm, :]

    def finish_pool(u):
        ext = jnp.concatenate([pool_carry[...], u], axis=0)
        pos = j * tm + lax.broadcasted_iota(jnp.int32, (tm, 1), 0)
        pooled = []
        for g, w in enumerate(POOL_WINDOWS):
            sl = slice(g * POOL_GROUP, (g + 1) * POOL_GROUP)
            win = ext[:, sl]
            step = 1
            while step < w:
                win = win + pltpu.roll(win, step, axis=0)
                step *= 2
            cnt = jnp.minimum(pos + 1, w).astype(F32)
            pooled.append((win[POOL_HALO:, :] / cnt - u[:, sl]).astype(BF16))
        for g in range(len(POOL_WINDOWS)):
            sl = slice(g * POOL_GROUP, (g + 1) * POOL_GROUP)
            mixed = jnp.dot(pooled[g], pmix_ref[g], preferred_element_type=F32)
            ob_ref[0, :, sl] = (mixed * pscale_ref[:, sl]).astype(BF16)
        pool_carry[...] = u[tm - POOL_HALO:, :]

    def finish_memory(qm_all):
        heads = [slice(hd * MEM_HEAD_DIM, (hd + 1) * MEM_HEAD_DIM) for hd in range(MEM_HEADS)]
        scores = [lax.dot_general(_rms(qm_all[:, sl], mqg_ref[...]).astype(BF16), km_ref[0, :, sl], _NT,
                                  preferred_element_type=F32) * (MEM_HEAD_DIM ** -0.5) for sl in heads]
        probs = [jnp.exp(s - jnp.max(s, axis=-1, keepdims=True)) for s in scores]
        for sl, p in zip(heads, probs):
            den = jnp.sum(p, axis=-1, keepdims=True)
            o = jnp.dot(p.astype(BF16), vm_ref[0, :, sl], preferred_element_type=F32)
            om_ref[0, :, sl] = (o / den).astype(BF16)

    stages = [
        (project(wqkv_ref, 0 * width, 1 * width), finish_heads(qa_ref, 0, rot=True, mul=scale)),
        (project(wqkv_ref, 1 * width, 2 * width), finish_heads(ka_ref, 1, rot=True, means=True)),
        (project(wqkv_ref, 2 * width, 3 * width), finish_heads(va_ref)),
        (project(wqkv_ref, 3 * width, 4 * width), finish_heads(qc_ref, 2, mul=scale)),
        (project(wqkv_ref, 4 * width, 5 * width), finish_heads(kc_ref, 3)),
        (project(wqkv_ref, 5 * width, 6 * width), finish_heads(vc_ref)),
        (project(wf_ref), finish_forget),
        (project(wp_ref), finish_pool),
        (project(wmq_ref), finish_memory),
    ]
    pending = None
    for matmul, finish in stages:
        y = matmul()
        if pending is not None:
            pending[0](pending[1])
        pending = (finish, y)
    pending[0](pending[1])


def _project(x3, norm_gain, w_in, b_forget, moba_q_gain, moba_k_gain, fox_q_gain, fox_k_gain,
             cos, sin, pool_mix, pool_scale, k_m, v_m, mem_q_gain):
    b, s, d = x3.shape
    width = MOBA_HEADS * HEAD_DIM
    pool_width = POOL_GROUP * len(POOL_WINDOWS)
    mem_width = MEM_HEADS * MEM_HEAD_DIM
    m_len = k_m.shape[1]
    tm = min(TM_PROJ, s)
    nt = s // tm
    o0 = 6 * width
    w_qkv = w_in[:, :o0].astype(BF16)
    w_f = jnp.pad(w_in[:, o0:o0 + FOX_HEADS], ((0, 0), (0, LANES - FOX_HEADS))).astype(BF16)
    b_f = jnp.pad(b_forget, (0, LANES - FOX_HEADS))[None, :]
    o1 = o0 + FOX_HEADS
    w_pool = w_in[:, o1:o1 + pool_width].astype(BF16)
    o2 = o1 + pool_width
    w_mq = w_in[:, o2:o2 + mem_width].astype(BF16)
    gains = jnp.stack([jnp.tile(g, HEADS_PER_GROUP)
                       for g in (moba_q_gain, moba_k_gain, fox_q_gain, fox_k_gain)])

    tile = lambda w: pl.BlockSpec((1, tm, w), lambda i, j: (i, j, 0))
    act = lambda w: jax.ShapeDtypeStruct((b, s, w), BF16)
    outs = pl.pallas_call(
        _proj_kernel,
        out_shape=(act(width),) * 6 + (jax.ShapeDtypeStruct((b, s, LANES), F32), act(pool_width),
                                       act(mem_width), jax.ShapeDtypeStruct((b, nt, 8, width), F32)),
        grid=(b, nt),
        in_specs=[tile(d), _resident((1, d)), _resident((d, o0)), _resident((d, LANES)),
                  _resident((d, pool_width)), _resident((d, mem_width)), _resident((1, LANES)),
                  _resident((4, LANES)), tile(LANES), tile(LANES),
                  _resident((len(POOL_WINDOWS), POOL_GROUP, POOL_GROUP)), _resident((1, pool_width)),
                  pl.BlockSpec((1, m_len, mem_width), lambda i, j: (i, 0, 0)),
                  pl.BlockSpec((1, m_len, mem_width), lambda i, j: (i, 0, 0)),
                  _resident((1, MEM_HEAD_DIM))],
        out_specs=(tile(width),) * 6 + (tile(LANES), tile(pool_width), tile(mem_width),
                                        pl.BlockSpec((1, 1, 8, width), lambda i, j: (i, j, 0, 0))),
        scratch_shapes=[pltpu.VMEM((POOL_HALO, pool_width), F32), pltpu.VMEM((1, LANES), F32)],
        compiler_params=pltpu.CompilerParams(dimension_semantics=("arbitrary", "arbitrary"),
                                             vmem_limit_bytes=VMEM_LIMIT),
        name="mixer_projection",
    )(x3, norm_gain[None, :], w_qkv, w_f, w_pool, w_mq, b_f, gains, cos, sin,
      pool_mix.astype(BF16), pool_scale[None, :], k_m, v_m, mem_q_gain[None, :])
    qa, ka, va, qc, kc, vc, c, ob, om, kmean = outs
    kmean = kmean[:, :, :tm // MOBA_BLOCK, :].reshape(b, s // MOBA_BLOCK, width)
    return qa, ka, va, qc, kc, vc, c, ob, om, kmean


def _stage_values(v_ref, vt_ref, r0, rows):
    vt = v_ref[0, pl.ds(r0, rows), :].T
    row = lax.broadcasted_iota(jnp.int32, (AUG_ROWS, rows), 0)
    tail = jnp.where(row == 0, 1.0, 0.0).astype(BF16)
    for hh in range(HEADS_PER_GROUP):
        vt_ref[hh, :, pl.ds(r0, rows)] = jnp.concatenate([vt[hh * HEAD_DIM:(hh + 1) * HEAD_DIM], tail], axis=0)


def _stage_keys(k_ref, kaug_ref, r0, rows, ext0, ext1):
    k = k_ref[0, pl.ds(r0, rows), :]
    lane = lax.broadcasted_iota(jnp.int32, k.shape, 1)
    kaug_ref[0, pl.ds(r0, rows), :] = jnp.where(lane < HEAD_DIM, k, ext0.astype(BF16))
    kaug_ref[1, pl.ds(r0, rows), :] = jnp.where(lane >= HEAD_DIM, k, ext1.astype(BF16))


def _augment_queries(qt, e0, e1):
    tq = qt.shape[1]
    pad = jnp.zeros((HEAD_DIM - AUG_ROWS, tq), F32)
    qa0 = jnp.concatenate([qt[:HEAD_DIM], e0, pad], axis=0).astype(BF16)
    qa1 = jnp.concatenate([e1, pad, qt[HEAD_DIM:]], axis=0).astype(BF16)
    return qa0, qa1


def _issue_scores(qa, kaug_ref, s_ref, jb, slot):
    k0 = pl.multiple_of(jb * TK_ATTN, TK_ATTN)
    for hh in range(HEADS_PER_GROUP):
        s_ref[slot, hh] = jnp.dot(kaug_ref[hh, pl.ds(k0, TK_ATTN), :], qa[hh], preferred_element_type=F32)


def _attend_pair(qa, next_queries, kaug_ref, vt_ref, s_ref, i, o_ref):
    tq = o_ref.shape[0]
    tk = TK_ATTN
    key_i = lax.broadcasted_iota(jnp.int32, (tk, tq), 0)
    qry_i = lax.broadcasted_iota(jnp.int32, (tk, tq), 1)

    def issue_scores(jb, slot):
        _issue_scores(qa, kaug_ref, s_ref, jb, slot)

    def block_max(slot, diag=None):
        out = []
        for hh in range(HEADS_PER_GROUP):
            s = s_ref[slot, hh]
            if diag is not None:
                s = jnp.where(key_i + diag * tk <= qry_i, s, NEG_INF)
            out.append(jnp.max(s, axis=0, keepdims=True))
        return tuple(out)

    def accumulate(jb, slot, carry, mx, diag=None):
        k0 = pl.multiple_of(jb * tk, tk)
        new = []
        for hh in range(HEADS_PER_GROUP):
            m, acc = carry[hh]
            s = s_ref[slot, hh]
            if diag is not None:
                s = jnp.where(key_i + diag * tk <= qry_i, s, NEG_INF)
            m_new = jnp.maximum(m, mx[hh])
            alpha = jnp.exp2(m - m_new)
            p = jnp.exp2(s - m_new).astype(BF16)
            vb = vt_ref[hh, :, pl.ds(k0, tk)]
            new.append((m_new, alpha * acc + jnp.dot(vb, p, preferred_element_type=F32)))
        return tuple(new)

    def step(t, state):
        carry, mx = state
        issue_scores(2 * t + 1, 1)
        carry = accumulate(2 * t, 0, carry, mx)
        mx = block_max(1)
        issue_scores(2 * t + 2, 0)
        carry = accumulate(2 * t + 1, 1, carry, mx)
        return carry, block_max(0)

    init = tuple((jnp.full((1, tq), NEG_INF, F32), jnp.zeros((V_ROWS, tq), F32))
                 for _ in range(HEADS_PER_GROUP))
    state = lax.fori_loop(0, i // 2, lambda u, st: step(2 * u + 1, step(2 * u, st)), (init, block_max(0)))
    carry, _ = lax.cond(i % 2 == 1, lambda st: step(i - 1, st), lambda st: st, state)
    half = tq // 2
    k_last = pl.multiple_of((2 * i + 1) * tk, tk)
    for hh in range(HEADS_PER_GROUP):
        s_ref[1, hh, :, half:] = jnp.dot(kaug_ref[hh, pl.ds(k_last, tk), :], qa[hh][:, half:],
                                         preferred_element_type=F32)
    qa_next = next_queries()
    carry = accumulate(2 * i, 0, carry, block_max(0, diag=0), diag=0)
    _issue_scores(qa_next, kaug_ref, s_ref, 0, 0)
    causal_half = (lax.broadcasted_iota(jnp.int32, (tk, half), 0)
                   <= lax.broadcasted_iota(jnp.int32, (tk, half), 1))
    accs = []
    for hh in range(HEADS_PER_GROUP):
        m, acc = carry[hh]
        s = jnp.where(causal_half, s_ref[1, hh, :, half:], NEG_INF)
        m_new = jnp.maximum(m[:, half:], jnp.max(s, axis=0, keepdims=True))
        alpha = jnp.exp2(m[:, half:] - m_new)
        p = jnp.exp2(s - m_new).astype(BF16)
        right = alpha * acc[:, half:] + jnp.dot(vt_ref[hh, :, pl.ds(k_last, tk)], p, preferred_element_type=F32)
        accs.append(jnp.concatenate([acc[:, :half], right], axis=1))
    ot = jnp.concatenate([acc[:HEAD_DIM] / acc[HEAD_DIM:HEAD_DIM + 1] for acc in accs], axis=0)
    o_ref[...] = ot.T.astype(o_ref.dtype)
    return qa_next


def _split3(x):
    hi = x.astype(BF16).astype(F32)
    r = x - hi
    mid = r.astype(BF16).astype(F32)
    return hi, mid, r - mid


def _fox_kernel(q_ref, k_ref, v_ref, c_ref, crow_ref, o_ref, kaug_ref, vt_ref, s_ref):
    g = pl.program_id(1)
    tq = TQ_ATTN
    s_len = k_ref.shape[1]
    n_parts = 3

    r = lax.broadcasted_iota(jnp.int32, (n_parts * LANES, 2 * LANES), 0)
    col = lax.broadcasted_iota(jnp.int32, (n_parts * LANES, 2 * LANES), 1)
    hit = jnp.zeros(r.shape, jnp.bool_)
    for part in range(n_parts):
        hit = hit | (((r == part * LANES + HEADS_PER_GROUP * g) & (col == HEAD_DIM + n_parts + part))
                     | ((r == part * LANES + HEADS_PER_GROUP * g + 1) & (col == LANES + n_parts + part)))
    place = jnp.where(hit, -1.0, 0.0).astype(BF16)
    lane2 = lax.broadcasted_iota(jnp.int32, (1, 2 * LANES), 1)
    ones = jnp.where(((lane2 >= HEAD_DIM) & (lane2 < HEAD_DIM + n_parts))
                     | ((lane2 >= LANES) & (lane2 < LANES + n_parts)), 1.0, 0.0)
    rows = min(STAGE_ROWS, s_len)
    per_trip = 2 if s_len % (2 * rows) == 0 else 1

    def stage(ci, _):
        for sub in range(per_trip):
            r0 = pl.multiple_of((ci * per_trip + sub) * rows, rows)
            _stage_values(v_ref, vt_ref, r0, rows)
            parts = jnp.concatenate(
                [p.astype(BF16) for p in _split3(c_ref[0, pl.ds(r0, rows), :] * LOG2E)], axis=1)
            ext = ones + jnp.dot(parts, place, preferred_element_type=F32)
            _stage_keys(k_ref, kaug_ref, r0, rows, ext[:, :LANES], ext[:, LANES:])
        return 0

    lax.fori_loop(0, s_len // (rows * per_trip), stage, 0)

    row = lax.broadcasted_iota(jnp.int32, (AUG_ROWS, tq), 0)

    def queries(i):
        q0 = pl.multiple_of(i * tq, tq)
        qt = q_ref[0, pl.ds(q0, tq), :].astype(F32).T
        exts = []
        for hh in range(HEADS_PER_GROUP):
            hi, mid, lo = _split3(crow_ref[0, 0, hh:hh + 1, pl.ds(q0, tq)] * LOG2E)
            e = jnp.where(row == 0, hi, jnp.where(row == 1, mid, jnp.where(row == 2, lo, 0.0)))
            exts.append(jnp.where((row >= n_parts) & (row < 2 * n_parts), 1.0, e))
        return _augment_queries(qt, *exts)

    _run_query_tiles(queries, kaug_ref, vt_ref, s_ref, o_ref, s_len // tq)


def _run_query_tiles(queries, kaug_ref, vt_ref, s_ref, o_ref, n_tiles):
    tq = TQ_ATTN

    def tile(i, qa):
        q0 = pl.multiple_of(i * tq, tq)
        nxt = lambda: queries(jnp.minimum(i + 1, n_tiles - 1))
        return _attend_pair(qa, nxt, kaug_ref, vt_ref, s_ref, i, o_ref.at[0, pl.ds(q0, tq), :])

    first = queries(0)
    _issue_scores(first, kaug_ref, s_ref, 0, 0)
    lax.fori_loop(0, n_tiles, tile, first)


def _attention_call(kernel_fn, name, q, k, v, extra_inputs, extra_specs):
    b, s, width = q.shape
    n_grp = width // LANES
    tq = TQ_ATTN
    assert tq == 2 * TK_ATTN and s % tq == 0
    seq = pl.BlockSpec((1, s, LANES), lambda bi, g: (bi, 0, g))
    return pl.pallas_call(
        kernel_fn,
        out_shape=jax.ShapeDtypeStruct((b, s, width), BF16),
        grid=(b, n_grp),
        in_specs=[seq, seq, seq] + extra_specs,
        out_specs=seq,
        scratch_shapes=[pltpu.VMEM((HEADS_PER_GROUP, s, LANES), BF16),
                        pltpu.VMEM((HEADS_PER_GROUP, V_ROWS, s), BF16),
                        pltpu.VMEM((2, HEADS_PER_GROUP, TK_ATTN, tq), F32)],
        compiler_params=_params(),
        name=name,
    )(q, k, v, *extra_inputs)


def _fox_attention(q, k, v, c):
    b, s, width = q.shape
    n_grp = width // LANES
    crow = c[:, :, :FOX_HEADS].transpose(0, 2, 1).reshape(b, n_grp, HEADS_PER_GROUP, s)
    return _attention_call(
        _fox_kernel, "forgetting_attention", q, k, v, (c, crow),
        [pl.BlockSpec((1, s, LANES), lambda bi, g: (bi, 0, 0)),
         pl.BlockSpec((1, 1, HEADS_PER_GROUP, s), lambda bi, g: (bi, g, 0, 0))])


def _moba_kernel(q_ref, k_ref, v_ref, kmean_ref, o_ref, kaug_ref, vt_ref, s_ref):
    tq = TQ_ATTN
    s_len = k_ref.shape[1]

    _stage_values(v_ref, vt_ref, 0, s_len)
    lane_k = lax.broadcasted_iota(jnp.int32, (MOBA_BLOCK, LANES), 1)
    lane_1 = lax.broadcasted_iota(jnp.int32, (1, LANES), 1)

    def stage_block(n, _):
        r0 = pl.multiple_of(n * MOBA_BLOCK, MOBA_BLOCK)
        k = k_ref[0, pl.ds(r0, MOBA_BLOCK), :]
        hot_hi = jnp.where(lane_1 == n + HEAD_DIM, 1.0, 0.0).astype(BF16)
        hot_lo = jnp.where(lane_1 == n, 1.0, 0.0).astype(BF16)
        kaug_ref[0, pl.ds(r0, MOBA_BLOCK), :] = jnp.where(lane_k < HEAD_DIM, k, hot_hi)
        kaug_ref[1, pl.ds(r0, MOBA_BLOCK), :] = jnp.where(lane_k >= HEAD_DIM, k, hot_lo)
        return 0

    lax.fori_loop(0, s_len // MOBA_BLOCK, stage_block, 0)

    kmean = kmean_ref[0]
    km_hi = kmean.astype(BF16)
    km_lo = (kmean - km_hi.astype(F32)).astype(BF16)
    zeros = jnp.zeros((HEAD_DIM, tq), F32)
    row = lax.broadcasted_iota(jnp.int32, (AUG_ROWS, tq), 0)
    row_f = row.astype(F32)
    qpos = lax.broadcasted_iota(jnp.int32, (AUG_ROWS, tq), 1)

    def queries(i):
        q0 = pl.multiple_of(i * tq, tq)
        qt = q_ref[0, pl.ds(q0, tq), :].astype(F32).T
        own = (q0 + qpos) // MOBA_BLOCK
        past = row < own
        exts = []
        for hh in range(HEADS_PER_GROUP):
            qz = (jnp.concatenate([qt[:HEAD_DIM], zeros], axis=0) if hh == 0
                  else jnp.concatenate([zeros, qt[HEAD_DIM:]], axis=0)).astype(BF16)
            gate = (jnp.dot(km_hi, qz, preferred_element_type=F32)
                    + jnp.dot(km_lo, qz, preferred_element_type=F32))
            gate = jnp.where(past, gate, NEG_INF)
            keep = row == own
            for _ in range(MOBA_TOPK):
                top = jnp.max(gate, axis=0, keepdims=True)
                idx = jnp.min(jnp.where(gate == top, row_f, float(AUG_ROWS)), axis=0, keepdims=True)
                pick = row_f == idx
                keep = keep | (pick & past)
                gate = jnp.where(pick, -jnp.inf, gate)
            exts.append(jnp.where(keep, 0.0, NEG_INF))
        return _augment_queries(qt, *exts)

    _run_query_tiles(queries, kaug_ref, vt_ref, s_ref, o_ref, s_len // tq)


def _moba_attention(q, k, v, kmean):
    b, s, width = q.shape
    n_blk = s // MOBA_BLOCK
    assert n_blk <= AUG_ROWS and TK_ATTN == MOBA_BLOCK
    kmean = jnp.pad(kmean, ((0, 0), (0, AUG_ROWS - n_blk), (0, 0)))
    return _attention_call(
        _moba_kernel, "moba_attention", q, k, v, (kmean,),
        [pl.BlockSpec((1, AUG_ROWS, LANES), lambda bi, g: (bi, 0, g))])


def _merge_kernel(x_ref, gn_ref, oa_ref, ob_ref, oc_ref, om_ref, wg_ref, wbr_ref, wo_ref, o_ref):
    x = x_ref[...]
    d = x.shape[1]
    h = _rms(x, gn_ref[...]).astype(BF16)
    mix = jnp.zeros_like(x)
    for n, br in enumerate((oa_ref, ob_ref, oc_ref, om_ref)):
        gate = jax.nn.sigmoid(jnp.dot(h, wg_ref[:, n * d:(n + 1) * d], preferred_element_type=F32))
        y = jnp.dot(br[...], wbr_ref[n], preferred_element_type=F32)
        mix = mix + gate * y
    o_ref[...] = x + jnp.dot(mix.astype(BF16), wo_ref[...], preferred_element_type=F32)


def _merge(x2, norm_gain, branches, w_gate, w_branch, w_out):
    t, d = x2.shape
    bw = w_branch.shape[1]
    tm = min(TM_FFN, t)
    row = lambda w: pl.BlockSpec((tm, w), lambda i: (i, 0))
    return pl.pallas_call(
        _merge_kernel,
        out_shape=jax.ShapeDtypeStruct((t, d), F32),
        grid=(t // tm,),
        in_specs=[row(d), _resident((1, d)), row(bw), row(bw), row(bw), row(bw),
                  _resident((d, N_BRANCH * d)), _resident((N_BRANCH, bw, d)), _resident((d, d))],
        out_specs=row(d),
        compiler_params=_params(),
        name="gated_merge",
    )(x2, norm_gain[None, :], *branches, w_gate.astype(BF16), w_branch.astype(BF16), w_out.astype(BF16))


def kernel(x, mem, positions, norm_ffn1, w_ffn1_in, w_ffn1_out, norm_mix, w_in, b_forget, moba_q_gain, moba_k_gain, fox_q_gain, fox_k_gain, norm_mem, w_mem_kv, mem_q_gain, mem_k_gain, pool_mix, pool_scale, w_branch, w_out, norm_ffn2, w_ffn2_in, w_ffn2_out):
    b, s, d = x.shape
    depth = norm_ffn1.shape[0]
    gate_off = w_in.shape[2] - N_BRANCH * d
    cos, sin = _rope_tables(positions)
    x2 = x.reshape(b * s, d)
    for l in range(depth):
        x2 = _ffn(x2, norm_ffn1[l], w_ffn1_in[l], w_ffn1_out[l])
        k_m, v_m = _mem_kv(mem, norm_mem[l], w_mem_kv[l], mem_k_gain[l])
        qa, ka, va, qc, kc, vc, c, ob, om, kmean = _project(
            x2.reshape(b, s, d), norm_mix[l], w_in[l, :, :gate_off], b_forget[l],
            moba_q_gain[l], moba_k_gain[l], fox_q_gain[l], fox_k_gain[l],
            cos, sin, pool_mix[l], pool_scale[l], k_m, v_m, mem_q_gain[l])
        oa = _moba_attention(qa, ka, va, kmean)
        oc = _fox_attention(qc, kc, vc, c)
        flat = lambda a: a.reshape(b * s, a.shape[2])
        x2 = _merge(x2, norm_mix[l], (flat(oa), flat(ob), flat(oc), flat(om)),
                    w_in[l, :, gate_off:], w_branch[l], w_out[l])
        x2 = _ffn(x2, norm_ffn2[l], w_ffn2_in[l], w_ffn2_out[l])
    return x2.reshape(b, s, d)
```

```python
import jax
import jax.numpy as jnp
from jax import lax
from jax.experimental import pallas as pl
from jax.experimental.pallas import tpu as pltpu

HEAD_DIM = 64
MOBA_HEADS = 8
MOBA_BLOCK = 256
MOBA_TOPK = 3
FOX_HEADS = 8
POOL_WINDOWS = (2, 4, 8, 16)
POOL_GROUP = 128
MEM_HEADS = 4
MEM_HEAD_DIM = 128
ROPE_THETA = 500000.0
ROT_DIM = HEAD_DIM // 4
N_BRANCH = 4
RMS_EPS = 1e-6
NEG_INF = -1e30

LANES = 128
SUBLANES = 8
HEADS_PER_GROUP = LANES // HEAD_DIM
POOL_HALO = 16
VMEM_LIMIT = 56 * 1024 * 1024

TM_FFN = 1024
TM_PROJ = 1024
TS_ROPE = 1024
TQ_ATTN = 512
TK_ATTN = 256
STAGE_ROWS = 512
AUG_ROWS = 16
FFN_CHUNK = 256

LOG2E = 1.4426950408889634
V_ROWS = HEAD_DIM + AUG_ROWS

F32 = jnp.float32
BF16 = jnp.bfloat16
_NT = (((1,), (1,)), ((), ()))


def _params():
    return pltpu.CompilerParams(vmem_limit_bytes=VMEM_LIMIT)


def _resident(shape):
    nd = len(shape)
    return pl.BlockSpec(shape, lambda *_: (0,) * nd, pipeline_mode=pl.Buffered(1))


def _rms(x, gain):
    ms = jnp.mean(x * x, axis=-1, keepdims=True)
    return x * lax.rsqrt(ms + RMS_EPS) * gain


def _head_pair_rms(x, gain):
    lane = lax.broadcasted_iota(jnp.int32, x.shape, 1)
    lo = lane < HEAD_DIM
    sq = x * x
    s_lo = jnp.sum(jnp.where(lo, sq, 0.0), axis=-1, keepdims=True)
    s_hi = jnp.sum(jnp.where(lo, 0.0, sq), axis=-1, keepdims=True)
    ms = jnp.where(lo, s_lo, s_hi) * (1.0 / HEAD_DIM)
    return x * lax.rsqrt(ms + RMS_EPS) * gain


def _rotate(x, cos, sin):
    lane = lax.broadcasted_iota(jnp.int32, x.shape, 1)
    half = ROT_DIM // 2
    fwd = pltpu.roll(x, LANES - half, axis=1)
    bwd = pltpu.roll(x, half, axis=1)
    partner = jnp.where((lane % HEAD_DIM) < half, fwd, bwd)
    return x * cos + partner * sin


def _rope_kernel(pos_ref, freq_ref, sign_ref, cos_ref, sin_ref):
    ang = freq_ref[...] * pos_ref[0].astype(F32)
    cos_r = jnp.cos(ang)
    sin_r = jnp.sin(ang) * sign_ref[...]
    ts = ang.shape[1]
    one = jnp.ones((HEAD_DIM - ROT_DIM, ts), F32)
    zero = jnp.zeros((HEAD_DIM - ROT_DIM, ts), F32)
    cos_ref[0] = jnp.concatenate([cos_r, one] * HEADS_PER_GROUP, axis=0).T
    sin_ref[0] = jnp.concatenate([sin_r, zero] * HEADS_PER_GROUP, axis=0).T


def _rope_tables(positions):
    b, s = positions.shape
    half = ROT_DIM // 2
    inv_freq = jnp.power(ROPE_THETA, -jnp.arange(half, dtype=F32) * (2.0 / ROT_DIM))
    freq = jnp.tile(inv_freq, 2)[:, None]
    sign = jnp.concatenate([-jnp.ones((half,), F32), jnp.ones((half,), F32)])[:, None]
    ts = min(s, TS_ROPE)
    return pl.pallas_call(
        _rope_kernel,
        out_shape=(jax.ShapeDtypeStruct((b, s, LANES), F32),) * 2,
        grid=(b, s // ts),
        in_specs=[pl.BlockSpec((1, 1, ts), lambda i, j: (i, 0, j)),
                  pl.BlockSpec((ROT_DIM, 1), lambda i, j: (0, 0)),
                  pl.BlockSpec((ROT_DIM, 1), lambda i, j: (0, 0))],
        out_specs=(pl.BlockSpec((1, ts, LANES), lambda i, j: (i, j, 0)),) * 2,
        compiler_params=_params(),
        name="rope_tables",
    )(positions.reshape(b, 1, s), freq, sign)


def _ffn_kernel(x_ref, g_ref, wg_ref, wu_ref, wo_ref, o_ref):
    x = x_ref[...]
    h = _rms(x, g_ref[...]).astype(BF16)
    d_ff = wg_ref.shape[1]
    acc = jnp.zeros_like(x)
    for c in range(d_ff // FFN_CHUNK):
        sl = slice(c * FFN_CHUNK, (c + 1) * FFN_CHUNK)
        g = jnp.dot(h, wg_ref[:, sl], preferred_element_type=F32)
        u = jnp.dot(h, wu_ref[:, sl], preferred_element_type=F32)
        a = (g * jax.nn.sigmoid(g) * u).astype(BF16)
        acc = acc + jnp.dot(a, wo_ref[sl, :], preferred_element_type=F32)
    o_ref[...] = x + 0.5 * acc


def _ffn(x2, gain, w_in, w_out):
    t, d = x2.shape
    d_ff = w_out.shape[0]
    wg = w_in[:, :d_ff].astype(BF16)
    wu = w_in[:, d_ff:].astype(BF16)
    wo = w_out.astype(BF16)
    tm = min(TM_FFN, t)
    return pl.pallas_call(
        _ffn_kernel,
        out_shape=jax.ShapeDtypeStruct((t, d), F32),
        grid=(t // tm,),
        in_specs=[pl.BlockSpec((tm, d), lambda i: (i, 0)),
                  _resident((1, d)), _resident((d, d_ff)), _resident((d, d_ff)), _resident((d_ff, d))],
        out_specs=pl.BlockSpec((tm, d), lambda i: (i, 0)),
        compiler_params=_params(),
        name="ffn_half_step",
    )(x2, gain[None, :], wg, wu, wo)


def _memkv_kernel(mem_ref, gn_ref, w_ref, kg_ref, k_ref, v_ref):
    width = k_ref.shape[2]
    mn = _rms(mem_ref[0], gn_ref[...]).astype(BF16)
    kv = jnp.dot(mn, w_ref[...], preferred_element_type=F32)
    for h in range(MEM_HEADS):
        sl = slice(h * MEM_HEAD_DIM, (h + 1) * MEM_HEAD_DIM)
        k_ref[0, :, sl] = _rms(kv[:, sl], kg_ref[...]).astype(BF16)
    v_ref[0] = kv[:, width:].astype(BF16)


def _mem_kv(mem, norm_gain, w_kv, k_gain):
    b, m, d = mem.shape
    width = MEM_HEADS * MEM_HEAD_DIM
    return pl.pallas_call(
        _memkv_kernel,
        out_shape=(jax.ShapeDtypeStruct((b, m, width), BF16),) * 2,
        grid=(b,),
        in_specs=[pl.BlockSpec((1, m, d), lambda i: (i, 0, 0)),
                  _resident((1, d)), _resident((d, 2 * width)), _resident((1, MEM_HEAD_DIM))],
        out_specs=(pl.BlockSpec((1, m, width), lambda i: (i, 0, 0)),) * 2,
        compiler_params=_params(),
        name="memory_kv",
    )(mem, norm_gain[None, :], w_kv.astype(BF16), k_gain[None, :])


def _proj_kernel(x_ref, gn_ref, wqkv_ref, wf_ref, wp_ref, wmq_ref, bf_ref, gains_ref,
                 cos_ref, sin_ref, pmix_ref, pscale_ref, km_ref, vm_ref, mqg_ref,
                 qa_ref, ka_ref, va_ref, qc_ref, kc_ref, vc_ref, c_ref, ob_ref, om_ref, kmean_ref,
                 pool_carry, c_carry):
    j = pl.program_id(1)
    tm = x_ref.shape[1]
    width = qa_ref.shape[2]
    n_grp = width // LANES

    @pl.when(j == 0)
    def _():
        pool_carry[...] = jnp.zeros_like(pool_carry)
        c_carry[...] = jnp.zeros_like(c_carry)

    h = _rms(x_ref[0], gn_ref[...]).astype(BF16)
    scale = HEAD_DIM ** -0.5 * LOG2E
    cos = cos_ref[0]
    sin = sin_ref[0]
    kmean_ref[...] = jnp.zeros_like(kmean_ref)

    def project(w_ref, lo=None, hi=None):
        return lambda: jnp.dot(h, w_ref[...] if lo is None else w_ref[:, lo:hi], preferred_element_type=F32)

    def finish_heads(out_ref, gain_row=None, rot=False, mul=None, means=False):
        def run(y):
            for g in range(n_grp):
                sl = slice(g * LANES, (g + 1) * LANES)
                t = y[:, sl]
                if gain_row is not None:
                    t = _head_pair_rms(t, gains_ref[gain_row:gain_row + 1, :])
                if rot:
                    t = _rotate(t, cos, sin)
                if means:
                    for sb in range(tm // MOBA_BLOCK):
                        kmean_ref[0, 0, sb:sb + 1, sl] = jnp.mean(
                            t[sb * MOBA_BLOCK:(sb + 1) * MOBA_BLOCK], axis=0, keepdims=True)
                if mul is not None:
                    t = t * mul
                out_ref[0, :, sl] = t.astype(BF16)
        return run

    def finish_forget(y):
        lf = jax.nn.log_sigmoid(y + bf_ref[...])
        row = lax.broadcasted_iota(jnp.int32, lf.shape, 0)
        step = 1
        while step < tm:
            lf = lf + jnp.where(row >= step, pltpu.roll(lf, step, axis=0), 0.0)
            step *= 2
        c = lf + c_carry[...]
        c_ref[0] = c
        c_carry[...] = c[tm - 1:tm, :]

    def finish_pool(u):
        ext = jnp.concatenate([pool_carry[...], u], axis=0)
        pos = j * tm + lax.broadcasted_iota(jnp.int32, (tm, 1), 0)
        pooled = []
        for g, w in enumerate(POOL_WINDOWS):
            sl = slice(g * POOL_GROUP, (g + 1) * POOL_GROUP)
            win = ext[:, sl]
            step = 1
            while step < w:
                win = win + pltpu.roll(win, step, axis=0)
                step *= 2
            cnt = jnp.minimum(pos + 1, w).astype(F32)
            pooled.append((win[POOL_HALO:, :] / cnt - u[:, sl]).astype(BF16))
        for g in range(len(POOL_WINDOWS)):
            sl = slice(g * POOL_GROUP, (g + 1) * POOL_GROUP)
            mixed = jnp.dot(pooled[g], pmix_ref[g], preferred_element_type=F32)
            ob_ref[0, :, sl] = (mixed * pscale_ref[:, sl]).astype(BF16)
        pool_carry[...] = u[tm - POOL_HALO:, :]

    def finish_memory(qm_all):
        heads = [slice(hd * MEM_HEAD_DIM, (hd + 1) * MEM_HEAD_DIM) for hd in range(MEM_HEADS)]
        scores = [lax.dot_general(_rms(qm_all[:, sl], mqg_ref[...]).astype(BF16), km_ref[0, :, sl], _NT,
                                  preferred_element_type=F32) * (MEM_HEAD_DIM ** -0.5) for sl in heads]
        probs = [jnp.exp(s - jnp.max(s, axis=-1, keepdims=True)) for s in scores]
        for sl, p in zip(heads, probs):
            den = jnp.sum(p, axis=-1, keepdims=True)
            o = jnp.dot(p.astype(BF16), vm_ref[0, :, sl], preferred_element_type=F32)
            om_ref[0, :, sl] = (o / den).astype(BF16)

    stages = [
        (project(wqkv_ref, 0 * width, 1 * width), finish_heads(qa_ref, 0, rot=True, mul=scale)),
        (project(wqkv_ref, 1 * width, 2 * width), finish_heads(ka_ref, 1, rot=True, means=True)),
        (project(wqkv_ref, 2 * width, 3 * width), finish_heads(va_ref)),
        (project(wqkv_ref, 3 * width, 4 * width), finish_heads(qc_ref, 2, mul=scale)),
        (project(wqkv_ref, 4 * width, 5 * width), finish_heads(kc_ref, 3)),
        (project(wqkv_ref, 5 * width, 6 * width), finish_heads(vc_ref)),
        (project(wf_ref), finish_forget),
        (project(wp_ref), finish_pool),
        (project(wmq_ref), finish_memory),
    ]
    pending = None
    for matmul, finish in stages:
        y = matmul()
        if pending is not None:
            pending[0](pending[1])
        pending = (finish, y)
    pending[0](pending[1])


def _project(x3, norm_gain, w_in, b_forget, moba_q_gain, moba_k_gain, fox_q_gain, fox_k_gain,
             cos, sin, pool_mix, pool_scale, k_m, v_m, mem_q_gain):
    b, s, d = x3.shape
    width = MOBA_HEADS * HEAD_DIM
    pool_width = POOL_GROUP * len(POOL_WINDOWS)
    mem_width = MEM_HEADS * MEM_HEAD_DIM
    m_len = k_m.shape[1]
    tm = min(TM_PROJ, s)
    nt = s // tm
    o0 = 6 * width
    w_qkv = w_in[:, :o0].astype(BF16)
    w_f = jnp.pad(w_in[:, o0:o0 + FOX_HEADS], ((0, 0), (0, LANES - FOX_HEADS))).astype(BF16)
    b_f = jnp.pad(b_forget, (0, LANES - FOX_HEADS))[None, :]
    o1 = o0 + FOX_HEADS
    w_pool = w_in[:, o1:o1 + pool_width].astype(BF16)
    o2 = o1 + pool_width
    w_mq = w_in[:, o2:o2 + mem_width].astype(BF16)
    gains = jnp.stack([jnp.tile(g, HEADS_PER_GROUP)
                       for g in (moba_q_gain, moba_k_gain, fox_q_gain, fox_k_gain)])

    tile = lambda w: pl.BlockSpec((1, tm, w), lambda i, j: (i, j, 0))
    act = lambda w: jax.ShapeDtypeStruct((b, s, w), BF16)
    outs = pl.pallas_call(
        _proj_kernel,
        out_shape=(act(width),) * 6 + (jax.ShapeDtypeStruct((b, s, LANES), F32), act(pool_width),
                                       act(mem_width), jax.ShapeDtypeStruct((b, nt, SUBLANES, width), F32)),
        grid=(b, nt),
        in_specs=[tile(d), _resident((1, d)), _resident((d, o0)), _resident((d, LANES)),
                  _resident((d, pool_width)), _resident((d, mem_width)), _resident((1, LANES)),
                  _resident((4, LANES)), tile(LANES), tile(LANES),
                  _resident((len(POOL_WINDOWS), POOL_GROUP, POOL_GROUP)), _resident((1, pool_width)),
                  pl.BlockSpec((1, m_len, mem_width), lambda i, j: (i, 0, 0)),
                  pl.BlockSpec((1, m_len, mem_width), lambda i, j: (i, 0, 0)),
                  _resident((1, MEM_HEAD_DIM))],
        out_specs=(tile(width),) * 6 + (tile(LANES), tile(pool_width), tile(mem_width),
                                        pl.BlockSpec((1, 1, SUBLANES, width), lambda i, j: (i, j, 0, 0))),
        scratch_shapes=[pltpu.VMEM((POOL_HALO, pool_width), F32), pltpu.VMEM((1, LANES), F32)],
        compiler_params=pltpu.CompilerParams(dimension_semantics=("arbitrary", "arbitrary"),
                                             vmem_limit_bytes=VMEM_LIMIT),
        name="mixer_projection",
    )(x3, norm_gain[None, :], w_qkv, w_f, w_pool, w_mq, b_f, gains, cos, sin,
      pool_mix.astype(BF16), pool_scale[None, :], k_m, v_m, mem_q_gain[None, :])
    qa, ka, va, qc, kc, vc, c, ob, om, kmean = outs
    kmean = kmean[:, :, :tm // MOBA_BLOCK, :].reshape(b, s // MOBA_BLOCK, width)
    return qa, ka, va, qc, kc, vc, c, ob, om, kmean


def _stage_values(v_ref, vt_ref, r0, rows):
    vt = v_ref[0, pl.ds(r0, rows), :].T
    row = lax.broadcasted_iota(jnp.int32, (AUG_ROWS, rows), 0)
    tail = jnp.where(row == 0, 1.0, 0.0).astype(BF16)
    for hh in range(HEADS_PER_GROUP):
        vt_ref[hh, :, pl.ds(r0, rows)] = jnp.concatenate([vt[hh * HEAD_DIM:(hh + 1) * HEAD_DIM], tail], axis=0)


def _stage_keys(k_ref, kaug_ref, r0, rows, ext0, ext1):
    k = k_ref[0, pl.ds(r0, rows), :]
    lane = lax.broadcasted_iota(jnp.int32, k.shape, 1)
    kaug_ref[0, pl.ds(r0, rows), :] = jnp.where(lane < HEAD_DIM, k, ext0.astype(BF16))
    kaug_ref[1, pl.ds(r0, rows), :] = jnp.where(lane >= HEAD_DIM, k, ext1.astype(BF16))


def _augment_queries(qt, e0, e1):
    tq = qt.shape[1]
    pad = jnp.zeros((HEAD_DIM - AUG_ROWS, tq), F32)
    qa0 = jnp.concatenate([qt[:HEAD_DIM], e0, pad], axis=0).astype(BF16)
    qa1 = jnp.concatenate([e1, pad, qt[HEAD_DIM:]], axis=0).astype(BF16)
    return qa0, qa1


def _issue_scores(qa, kaug_ref, s_ref, jb, slot):
    k0 = pl.multiple_of(jb * TK_ATTN, TK_ATTN)
    for hh in range(HEADS_PER_GROUP):
        s_ref[slot, hh] = jnp.dot(kaug_ref[hh, pl.ds(k0, TK_ATTN), :], qa[hh], preferred_element_type=F32)


def _attend_pair(qa, next_queries, kaug_ref, vt_ref, s_ref, i, o_ref):
    tq = o_ref.shape[0]
    tk = TK_ATTN
    key_i = lax.broadcasted_iota(jnp.int32, (tk, tq), 0)
    qry_i = lax.broadcasted_iota(jnp.int32, (tk, tq), 1)

    def issue_scores(jb, slot):
        _issue_scores(qa, kaug_ref, s_ref, jb, slot)

    def block_max(slot, causal=False):
        out = []
        for hh in range(HEADS_PER_GROUP):
            s = s_ref[slot, hh]
            if causal:
                s = jnp.where(key_i <= qry_i, s, NEG_INF)
            out.append(jnp.max(s, axis=0, keepdims=True))
        return tuple(out)

    def accumulate(jb, slot, carry, mx, causal=False):
        k0 = pl.multiple_of(jb * tk, tk)
        new = []
        for hh in range(HEADS_PER_GROUP):
            m, acc = carry[hh]
            s = s_ref[slot, hh]
            if causal:
                s = jnp.where(key_i <= qry_i, s, NEG_INF)
            m_new = jnp.maximum(m, mx[hh])
            alpha = jnp.exp2(m - m_new)
            p = jnp.exp2(s - m_new).astype(BF16)
            vb = vt_ref[hh, :, pl.ds(k0, tk)]
            new.append((m_new, alpha * acc + jnp.dot(vb, p, preferred_element_type=F32)))
        return tuple(new)

    def step(t, state):
        carry, mx = state
        issue_scores(2 * t + 1, 1)
        carry = accumulate(2 * t, 0, carry, mx)
        mx = block_max(1)
        issue_scores(2 * t + 2, 0)
        carry = accumulate(2 * t + 1, 1, carry, mx)
        return carry, block_max(0)

    init = tuple((jnp.full((1, tq), NEG_INF, F32), jnp.zeros((V_ROWS, tq), F32))
                 for _ in range(HEADS_PER_GROUP))
    def steps(t0, n, st):
        for d in range(n):
            st = step(t0 + d, st)
        return st

    state = lax.fori_loop(0, i // 4, lambda u, st: steps(4 * u, 4, st), (init, block_max(0)))
    done = (i // 4) * 4
    state = lax.cond(i - done >= 2, lambda st: steps(done, 2, st), lambda st: st, state)
    carry, _ = lax.cond(i % 2 == 1, lambda st: step(i - 1, st), lambda st: st, state)
    half = tq // 2
    k_last = pl.multiple_of((2 * i + 1) * tk, tk)
    for hh in range(HEADS_PER_GROUP):
        s_ref[1, hh, :, half:] = jnp.dot(kaug_ref[hh, pl.ds(k_last, tk), :], qa[hh][:, half:],
                                         preferred_element_type=F32)
    qa_next = next_queries()
    carry = accumulate(2 * i, 0, carry, block_max(0, causal=True), causal=True)
    _issue_scores(qa_next, kaug_ref, s_ref, 0, 0)
    causal_half = (lax.broadcasted_iota(jnp.int32, (tk, half), 0)
                   <= lax.broadcasted_iota(jnp.int32, (tk, half), 1))
    accs = []
    for hh in range(HEADS_PER_GROUP):
        m, acc = carry[hh]
        s = jnp.where(causal_half, s_ref[1, hh, :, half:], NEG_INF)
        m_new = jnp.maximum(m[:, half:], jnp.max(s, axis=0, keepdims=True))
        alpha = jnp.exp2(m[:, half:] - m_new)
        p = jnp.exp2(s - m_new).astype(BF16)
        right = alpha * acc[:, half:] + jnp.dot(vt_ref[hh, :, pl.ds(k_last, tk)], p, preferred_element_type=F32)
        accs.append(jnp.concatenate([acc[:, :half], right], axis=1))
    ot = jnp.concatenate([acc[:HEAD_DIM] / acc[HEAD_DIM:HEAD_DIM + 1] for acc in accs], axis=0)
    o_ref[...] = ot.T.astype(o_ref.dtype)
    return qa_next


def _split3(x):
    hi = x.astype(BF16).astype(F32)
    r = x - hi
    mid = r.astype(BF16).astype(F32)
    return hi, mid, r - mid


def _fox_kernel(q_ref, k_ref, v_ref, c_ref, crow_ref, o_ref, kaug_ref, vt_ref, s_ref):
    g = pl.program_id(1)
    tq = TQ_ATTN
    s_len = k_ref.shape[1]
    n_parts = 3

    r = lax.broadcasted_iota(jnp.int32, (n_parts * LANES, 2 * LANES), 0)
    col = lax.broadcasted_iota(jnp.int32, (n_parts * LANES, 2 * LANES), 1)
    hit = jnp.zeros(r.shape, jnp.bool_)
    for part in range(n_parts):
        hit = hit | (((r == part * LANES + HEADS_PER_GROUP * g) & (col == HEAD_DIM + n_parts + part))
                     | ((r == part * LANES + HEADS_PER_GROUP * g + 1) & (col == LANES + n_parts + part)))
    place = jnp.where(hit, -1.0, 0.0).astype(BF16)
    lane2 = lax.broadcasted_iota(jnp.int32, (1, 2 * LANES), 1)
    ones = jnp.where(((lane2 >= HEAD_DIM) & (lane2 < HEAD_DIM + n_parts))
                     | ((lane2 >= LANES) & (lane2 < LANES + n_parts)), 1.0, 0.0)
    rows = min(STAGE_ROWS, s_len)
    per_trip = 2 if s_len % (2 * rows) == 0 else 1

    def stage(ci, _):
        for sub in range(per_trip):
            r0 = pl.multiple_of((ci * per_trip + sub) * rows, rows)
            _stage_values(v_ref, vt_ref, r0, rows)
            parts = jnp.concatenate(
                [p.astype(BF16) for p in _split3(c_ref[0, pl.ds(r0, rows), :] * LOG2E)], axis=1)
            ext = ones + jnp.dot(parts, place, preferred_element_type=F32)
            _stage_keys(k_ref, kaug_ref, r0, rows, ext[:, :LANES], ext[:, LANES:])
        return 0

    lax.fori_loop(0, s_len // (rows * per_trip), stage, 0)

    row = lax.broadcasted_iota(jnp.int32, (AUG_ROWS, tq), 0)

    def queries(i):
        q0 = pl.multiple_of(i * tq, tq)
        qt = q_ref[0, pl.ds(q0, tq), :].astype(F32).T
        exts = []
        for hh in range(HEADS_PER_GROUP):
            hi, mid, lo = _split3(crow_ref[0, 0, hh:hh + 1, pl.ds(q0, tq)] * LOG2E)
            e = jnp.where(row == 0, hi, jnp.where(row == 1, mid, jnp.where(row == 2, lo, 0.0)))
            exts.append(jnp.where((row >= n_parts) & (row < 2 * n_parts), 1.0, e))
        return _augment_queries(qt, *exts)

    _run_query_tiles(queries, kaug_ref, vt_ref, s_ref, o_ref, s_len // tq)


def _run_query_tiles(queries, kaug_ref, vt_ref, s_ref, o_ref, n_tiles):
    tq = TQ_ATTN

    def tile(i, qa):
        q0 = pl.multiple_of(i * tq, tq)
        nxt = lambda: queries(jnp.minimum(i + 1, n_tiles - 1))
        return _attend_pair(qa, nxt, kaug_ref, vt_ref, s_ref, i, o_ref.at[0, pl.ds(q0, tq), :])

    first = queries(0)
    _issue_scores(first, kaug_ref, s_ref, 0, 0)
    lax.fori_loop(0, n_tiles, tile, first)


def _attention_call(kernel_fn, name, q, k, v, extra_inputs, extra_specs):
    b, s, width = q.shape
    n_grp = width // LANES
    tq = TQ_ATTN
    assert tq == 2 * TK_ATTN and s % tq == 0
    seq = pl.BlockSpec((1, s, LANES), lambda bi, g: (bi, 0, g))
    return pl.pallas_call(
        kernel_fn,
        out_shape=jax.ShapeDtypeStruct((b, s, width), BF16),
        grid=(b, n_grp),
        in_specs=[seq, seq, seq] + extra_specs,
        out_specs=seq,
        scratch_shapes=[pltpu.VMEM((HEADS_PER_GROUP, s, LANES), BF16),
                        pltpu.VMEM((HEADS_PER_GROUP, V_ROWS, s), BF16),
                        pltpu.VMEM((2, HEADS_PER_GROUP, TK_ATTN, tq), F32)],
        compiler_params=_params(),
        name=name,
    )(q, k, v, *extra_inputs)


def _fox_attention(q, k, v, c):
    b, s, width = q.shape
    n_grp = width // LANES
    crow = c[:, :, :FOX_HEADS].transpose(0, 2, 1).reshape(b, n_grp, HEADS_PER_GROUP, s)
    return _attention_call(
        _fox_kernel, "forgetting_attention", q, k, v, (c, crow),
        [pl.BlockSpec((1, s, LANES), lambda bi, g: (bi, 0, 0)),
         pl.BlockSpec((1, 1, HEADS_PER_GROUP, s), lambda bi, g: (bi, g, 0, 0))])


def _moba_kernel(q_ref, k_ref, v_ref, kmean_ref, o_ref, kaug_ref, vt_ref, s_ref):
    tq = TQ_ATTN
    s_len = k_ref.shape[1]

    _stage_values(v_ref, vt_ref, 0, s_len)
    lane_k = lax.broadcasted_iota(jnp.int32, (MOBA_BLOCK, LANES), 1)
    lane_1 = lax.broadcasted_iota(jnp.int32, (1, LANES), 1)

    def stage_block(n, _):
        r0 = pl.multiple_of(n * MOBA_BLOCK, MOBA_BLOCK)
        k = k_ref[0, pl.ds(r0, MOBA_BLOCK), :]
        hot_hi = jnp.where(lane_1 == n + HEAD_DIM, 1.0, 0.0).astype(BF16)
        hot_lo = jnp.where(lane_1 == n, 1.0, 0.0).astype(BF16)
        kaug_ref[0, pl.ds(r0, MOBA_BLOCK), :] = jnp.where(lane_k < HEAD_DIM, k, hot_hi)
        kaug_ref[1, pl.ds(r0, MOBA_BLOCK), :] = jnp.where(lane_k >= HEAD_DIM, k, hot_lo)
        return 0

    lax.fori_loop(0, s_len // MOBA_BLOCK, stage_block, 0)

    kmean = kmean_ref[0]
    km_hi = kmean.astype(BF16)
    km_lo = (kmean - km_hi.astype(F32)).astype(BF16)
    zeros = jnp.zeros((HEAD_DIM, tq), F32)
    row = lax.broadcasted_iota(jnp.int32, (AUG_ROWS, tq), 0)
    row_f = row.astype(F32)
    qpos = lax.broadcasted_iota(jnp.int32, (AUG_ROWS, tq), 1)

    def queries(i):
        q0 = pl.multiple_of(i * tq, tq)
        qt = q_ref[0, pl.ds(q0, tq), :].astype(F32).T
        own = (q0 + qpos) // MOBA_BLOCK
        past = row < own
        exts = []
        for hh in range(HEADS_PER_GROUP):
            qz = (jnp.concatenate([qt[:HEAD_DIM], zeros], axis=0) if hh == 0
                  else jnp.concatenate([zeros, qt[HEAD_DIM:]], axis=0)).astype(BF16)
            gate = (jnp.dot(km_hi, qz, preferred_element_type=F32)
                    + jnp.dot(km_lo, qz, preferred_element_type=F32))
            gate = jnp.where(past, gate, NEG_INF)
            keep = row == own
            for _ in range(MOBA_TOPK):
                top = jnp.max(gate, axis=0, keepdims=True)
                idx = jnp.min(jnp.where(gate == top, row_f, float(AUG_ROWS)), axis=0, keepdims=True)
                pick = row_f == idx
                keep = keep | (pick & past)
                gate = jnp.where(pick, -jnp.inf, gate)
            exts.append(jnp.where(keep, 0.0, NEG_INF))
        return _augment_queries(qt, *exts)

    _run_query_tiles(queries, kaug_ref, vt_ref, s_ref, o_ref, s_len // tq)


def _moba_attention(q, k, v, kmean):
    b, s, width = q.shape
    n_blk = s // MOBA_BLOCK
    assert n_blk <= AUG_ROWS and TK_ATTN == MOBA_BLOCK
    kmean = jnp.pad(kmean, ((0, 0), (0, AUG_ROWS - n_blk), (0, 0)))
    return _attention_call(
        _moba_kernel, "moba_attention", q, k, v, (kmean,),
        [pl.BlockSpec((1, AUG_ROWS, LANES), lambda bi, g: (bi, 0, g))])


def _merge_kernel(x_ref, gn_ref, oa_ref, ob_ref, oc_ref, om_ref, wg_ref, wbr_ref, wo_ref, o_ref):
    x = x_ref[...]
    d = x.shape[1]
    h = _rms(x, gn_ref[...]).astype(BF16)
    mix = jnp.zeros_like(x)
    for n, br in enumerate((oa_ref, ob_ref, oc_ref, om_ref)):
        gate = jax.nn.sigmoid(jnp.dot(h, wg_ref[:, n * d:(n + 1) * d], preferred_element_type=F32))
        y = jnp.dot(br[...], wbr_ref[n], preferred_element_type=F32)
        mix = mix + gate * y
    o_ref[...] = x + jnp.dot(mix.astype(BF16), wo_ref[...], preferred_element_type=F32)


def _merge(x2, norm_gain, branches, w_gate, w_branch, w_out):
    t, d = x2.shape
    bw = w_branch.shape[1]
    tm = min(TM_FFN, t)
    row = lambda w: pl.BlockSpec((tm, w), lambda i: (i, 0))
    return pl.pallas_call(
        _merge_kernel,
        out_shape=jax.ShapeDtypeStruct((t, d), F32),
        grid=(t // tm,),
        in_specs=[row(d), _resident((1, d)), row(bw), row(bw), row(bw), row(bw),
                  _resident((d, N_BRANCH * d)), _resident((N_BRANCH, bw, d)), _resident((d, d))],
        out_specs=row(d),
        compiler_params=_params(),
        name="gated_merge",
    )(x2, norm_gain[None, :], *branches, w_gate.astype(BF16), w_branch.astype(BF16), w_out.astype(BF16))


def kernel(x, mem, positions, norm_ffn1, w_ffn1_in, w_ffn1_out, norm_mix, w_in, b_forget, moba_q_gain, moba_k_gain, fox_q_gain, fox_k_gain, norm_mem, w_mem_kv, mem_q_gain, mem_k_gain, pool_mix, pool_scale, w_branch, w_out, norm_ffn2, w_ffn2_in, w_ffn2_out):
    b, s, d = x.shape
    depth = norm_ffn1.shape[0]
    gate_off = w_in.shape[2] - N_BRANCH * d
    cos, sin = _rope_tables(positions)
    x2 = x.reshape(b * s, d)
    for l in range(depth):
        x2 = _ffn(x2, norm_ffn1[l], w_ffn1_in[l], w_ffn1_out[l])
        k_m, v_m = _mem_kv(mem, norm_mem[l], w_mem_kv[l], mem_k_gain[l])
        qa, ka, va, qc, kc, vc, c, ob, om, kmean = _project(
            x2.reshape(b, s, d), norm_mix[l], w_in[l, :, :gate_off], b_forget[l],
            moba_q_gain[l], moba_k_gain[l], fox_q_gain[l], fox_k_gain[l],
            cos, sin, pool_mix[l], pool_scale[l], k_m, v_m, mem_q_gain[l])
        oa = _moba_attention(qa, ka, va, kmean)
        oc = _fox_attention(qc, kc, vc, c)
        flat = lambda a: a.reshape(b * s, a.shape[2])
        x2 = _merge(x2, norm_mix[l], (flat(oa), flat(ob), flat(oc), flat(om)),
                    w_in[l, :, gate_off:], w_branch[l], w_out[l])
        x2 = _ffn(x2, norm_ffn2[l], w_ffn2_in[l], w_ffn2_out[l])
    return x2.reshape(b, s, d)
```

```python
import jax
import jax.numpy as jnp
from jax import lax
from jax.experimental import pallas as pl
from jax.experimental.pallas import tpu as pltpu

HEAD_DIM = 64
MOBA_HEADS = 8
MOBA_BLOCK = 256
MOBA_TOPK = 3
FOX_HEADS = 8
POOL_WINDOWS = (2, 4, 8, 16)
POOL_GROUP = 128
MEM_HEADS = 4
MEM_HEAD_DIM = 128
ROPE_THETA = 500000.0
ROT_DIM = HEAD_DIM // 4
N_BRANCH = 4
RMS_EPS = 1e-6
NEG_INF = -1e30

LANES = 128
SUBLANES = 8
HEADS_PER_GROUP = LANES // HEAD_DIM
POOL_HALO = 16
VMEM_LIMIT = 56 * 1024 * 1024

TM_FFN = 1024
TM_PROJ = 1024
TS_ROPE = 1024
TQ_ATTN = 512
TK_ATTN = 256
STAGE_ROWS = 512
AUG_ROWS = 16
FFN_CHUNK = 256

LOG2E = 1.4426950408889634
V_ROWS = HEAD_DIM + AUG_ROWS

F32 = jnp.float32
BF16 = jnp.bfloat16
_NT = (((1,), (1,)), ((), ()))


def _params():
    return pltpu.CompilerParams(vmem_limit_bytes=VMEM_LIMIT)


def _resident(shape):
    nd = len(shape)
    return pl.BlockSpec(shape, lambda *_: (0,) * nd, pipeline_mode=pl.Buffered(1))


def _rms(x, gain):
    ms = jnp.mean(x * x, axis=-1, keepdims=True)
    return x * lax.rsqrt(ms + RMS_EPS) * gain


def _head_pair_rms(x, gain):
    lane = lax.broadcasted_iota(jnp.int32, x.shape, 1)
    lo = lane < HEAD_DIM
    sq = x * x
    s_lo = jnp.sum(jnp.where(lo, sq, 0.0), axis=-1, keepdims=True)
    s_hi = jnp.sum(jnp.where(lo, 0.0, sq), axis=-1, keepdims=True)
    ms = jnp.where(lo, s_lo, s_hi) * (1.0 / HEAD_DIM)
    return x * lax.rsqrt(ms + RMS_EPS) * gain


def _rotate(x, cos, sin):
    lane = lax.broadcasted_iota(jnp.int32, x.shape, 1)
    half = ROT_DIM // 2
    fwd = pltpu.roll(x, LANES - half, axis=1)
    bwd = pltpu.roll(x, half, axis=1)
    partner = jnp.where((lane % HEAD_DIM) < half, fwd, bwd)
    return x * cos + partner * sin


def _rope_kernel(pos_ref, freq_ref, sign_ref, cos_ref, sin_ref):
    ang = freq_ref[...] * pos_ref[0].astype(F32)
    cos_r = jnp.cos(ang)
    sin_r = jnp.sin(ang) * sign_ref[...]
    ts = ang.shape[1]
    one = jnp.ones((HEAD_DIM - ROT_DIM, ts), F32)
    zero = jnp.zeros((HEAD_DIM - ROT_DIM, ts), F32)
    cos_ref[0] = jnp.concatenate([cos_r, one] * HEADS_PER_GROUP, axis=0).T
    sin_ref[0] = jnp.concatenate([sin_r, zero] * HEADS_PER_GROUP, axis=0).T


def _rope_tables(positions):
    b, s = positions.shape
    half = ROT_DIM // 2
    inv_freq = jnp.power(ROPE_THETA, -jnp.arange(half, dtype=F32) * (2.0 / ROT_DIM))
    freq = jnp.tile(inv_freq, 2)[:, None]
    sign = jnp.concatenate([-jnp.ones((half,), F32), jnp.ones((half,), F32)])[:, None]
    ts = min(s, TS_ROPE)
    return pl.pallas_call(
        _rope_kernel,
        out_shape=(jax.ShapeDtypeStruct((b, s, LANES), F32),) * 2,
        grid=(b, s // ts),
        in_specs=[pl.BlockSpec((1, 1, ts), lambda i, j: (i, 0, j)),
                  pl.BlockSpec((ROT_DIM, 1), lambda i, j: (0, 0)),
                  pl.BlockSpec((ROT_DIM, 1), lambda i, j: (0, 0))],
        out_specs=(pl.BlockSpec((1, ts, LANES), lambda i, j: (i, j, 0)),) * 2,
        compiler_params=_params(),
        name="rope_tables",
    )(positions.reshape(b, 1, s), freq, sign)


def _ffn_kernel(x_ref, g_ref, wg_ref, wu_ref, wo_ref, o_ref):
    x = x_ref[...]
    h = _rms(x, g_ref[...]).astype(BF16)
    d_ff = wg_ref.shape[1]
    acc = jnp.zeros_like(x)
    for c in range(d_ff // FFN_CHUNK):
        sl = slice(c * FFN_CHUNK, (c + 1) * FFN_CHUNK)
        g = jnp.dot(h, wg_ref[:, sl], preferred_element_type=F32)
        u = jnp.dot(h, wu_ref[:, sl], preferred_element_type=F32)
        a = (g * jax.nn.sigmoid(g) * u).astype(BF16)
        acc = acc + jnp.dot(a, wo_ref[sl, :], preferred_element_type=F32)
    o_ref[...] = x + 0.5 * acc


def _ffn(x2, gain, w_in, w_out):
    t, d = x2.shape
    d_ff = w_out.shape[0]
    wg = w_in[:, :d_ff].astype(BF16)
    wu = w_in[:, d_ff:].astype(BF16)
    wo = w_out.astype(BF16)
    tm = min(TM_FFN, t)
    return pl.pallas_call(
        _ffn_kernel,
        out_shape=jax.ShapeDtypeStruct((t, d), F32),
        grid=(t // tm,),
        in_specs=[pl.BlockSpec((tm, d), lambda i: (i, 0)),
                  _resident((1, d)), _resident((d, d_ff)), _resident((d, d_ff)), _resident((d_ff, d))],
        out_specs=pl.BlockSpec((tm, d), lambda i: (i, 0)),
        compiler_params=_params(),
        name="ffn_half_step",
    )(x2, gain[None, :], wg, wu, wo)


def _memkv_kernel(mem_ref, gn_ref, w_ref, kg_ref, k_ref, v_ref):
    width = k_ref.shape[2]
    mn = _rms(mem_ref[0], gn_ref[...]).astype(BF16)
    kv = jnp.dot(mn, w_ref[...], preferred_element_type=F32)
    for h in range(MEM_HEADS):
        sl = slice(h * MEM_HEAD_DIM, (h + 1) * MEM_HEAD_DIM)
        k_ref[0, :, sl] = _rms(kv[:, sl], kg_ref[...]).astype(BF16)
    v_ref[0] = kv[:, width:].astype(BF16)


def _mem_kv(mem, norm_gain, w_kv, k_gain):
    b, m, d = mem.shape
    width = MEM_HEADS * MEM_HEAD_DIM
    return pl.pallas_call(
        _memkv_kernel,
        out_shape=(jax.ShapeDtypeStruct((b, m, width), BF16),) * 2,
        grid=(b,),
        in_specs=[pl.BlockSpec((1, m, d), lambda i: (i, 0, 0)),
                  _resident((1, d)), _resident((d, 2 * width)), _resident((1, MEM_HEAD_DIM))],
        out_specs=(pl.BlockSpec((1, m, width), lambda i: (i, 0, 0)),) * 2,
        compiler_params=_params(),
        name="memory_kv",
    )(mem, norm_gain[None, :], w_kv.astype(BF16), k_gain[None, :])


def _proj_kernel(x_ref, gn_ref, wqkv_ref, wf_ref, wp_ref, wmq_ref, bf_ref, gains_ref,
                 cos_ref, sin_ref, pmix_ref, pscale_ref, km_ref, vm_ref, mqg_ref,
                 qa_ref, ka_ref, va_ref, qc_ref, kc_ref, vc_ref, c_ref, ob_ref, om_ref, kmean_ref,
                 pool_carry, c_carry):
    j = pl.program_id(1)
    tm = x_ref.shape[1]
    width = qa_ref.shape[2]
    n_grp = width // LANES

    @pl.when(j == 0)
    def _():
        pool_carry[...] = jnp.zeros_like(pool_carry)
        c_carry[...] = jnp.zeros_like(c_carry)

    h = _rms(x_ref[0], gn_ref[...]).astype(BF16)
    scale = HEAD_DIM ** -0.5 * LOG2E
    cos = cos_ref[0]
    sin = sin_ref[0]
    kmean_ref[...] = jnp.zeros_like(kmean_ref)

    def project(w_ref, lo=None, hi=None):
        return lambda: jnp.dot(h, w_ref[...] if lo is None else w_ref[:, lo:hi], preferred_element_type=F32)

    def finish_heads(out_ref, gain_row=None, rot=False, mul=None, means=False):
        def run(y):
            for g in range(n_grp):
                sl = slice(g * LANES, (g + 1) * LANES)
                t = y[:, sl]
                if gain_row is not None:
                    t = _head_pair_rms(t, gains_ref[gain_row:gain_row + 1, :])
                if rot:
                    t = _rotate(t, cos, sin)
                if means:
                    for sb in range(tm // MOBA_BLOCK):
                        kmean_ref[0, 0, sb:sb + 1, sl] = jnp.mean(
                            t[sb * MOBA_BLOCK:(sb + 1) * MOBA_BLOCK], axis=0, keepdims=True)
                if mul is not None:
                    t = t * mul
                out_ref[0, :, sl] = t.astype(BF16)
        return run

    def finish_forget(y):
        lf = jax.nn.log_sigmoid(y + bf_ref[...])
        row = lax.broadcasted_iota(jnp.int32, lf.shape, 0)
        step = 1
        while step < tm:
            lf = lf + jnp.where(row >= step, pltpu.roll(lf, step, axis=0), 0.0)
            step *= 2
        c = lf + c_carry[...]
        c_ref[0] = c
        c_carry[...] = c[tm - 1:tm, :]

    def finish_pool(u):
        ext = jnp.concatenate([pool_carry[...], u], axis=0)
        pos = j * tm + lax.broadcasted_iota(jnp.int32, (tm, 1), 0)
        pooled = []
        for g, w in enumerate(POOL_WINDOWS):
            sl = slice(g * POOL_GROUP, (g + 1) * POOL_GROUP)
            win = ext[:, sl]
            step = 1
            while step < w:
                win = win + pltpu.roll(win, step, axis=0)
                step *= 2
            cnt = jnp.minimum(pos + 1, w).astype(F32)
            pooled.append((win[POOL_HALO:, :] / cnt - u[:, sl]).astype(BF16))
        for g in range(len(POOL_WINDOWS)):
            sl = slice(g * POOL_GROUP, (g + 1) * POOL_GROUP)
            mixed = jnp.dot(pooled[g], pmix_ref[g], preferred_element_type=F32)
            ob_ref[0, :, sl] = (mixed * pscale_ref[:, sl]).astype(BF16)
        pool_carry[...] = u[tm - POOL_HALO:, :]

    def finish_memory(qm_all):
        heads = [slice(hd * MEM_HEAD_DIM, (hd + 1) * MEM_HEAD_DIM) for hd in range(MEM_HEADS)]
        scores = [lax.dot_general(_rms(qm_all[:, sl], mqg_ref[...]).astype(BF16), km_ref[0, :, sl], _NT,
                                  preferred_element_type=F32) * (MEM_HEAD_DIM ** -0.5) for sl in heads]
        probs = [jnp.exp(s - jnp.max(s, axis=-1, keepdims=True)) for s in scores]
        for sl, p in zip(heads, probs):
            den = jnp.sum(p, axis=-1, keepdims=True)
            o = jnp.dot(p.astype(BF16), vm_ref[0, :, sl], preferred_element_type=F32)
            om_ref[0, :, sl] = (o / den).astype(BF16)

    stages = [
        (project(wqkv_ref, 0 * width, 1 * width), finish_heads(qa_ref, 0, rot=True, mul=scale)),
        (project(wp_ref), finish_pool),
        (project(wqkv_ref, 1 * width, 2 * width), finish_heads(ka_ref, 1, rot=True, means=True)),
        (project(wf_ref), finish_forget),
        (project(wqkv_ref, 2 * width, 3 * width), finish_heads(va_ref)),
        (project(wqkv_ref, 3 * width, 4 * width), finish_heads(qc_ref, 2, mul=scale)),
        (project(wmq_ref), finish_memory),
        (project(wqkv_ref, 4 * width, 5 * width), finish_heads(kc_ref, 3)),
        (project(wqkv_ref, 5 * width, 6 * width), finish_heads(vc_ref)),
    ]
    pending = None
    for matmul, finish in stages:
        y = matmul()
        if pending is not None:
            pending[0](pending[1])
        pending = (finish, y)
    pending[0](pending[1])


def _project(x3, norm_gain, w_in, b_forget, moba_q_gain, moba_k_gain, fox_q_gain, fox_k_gain,
             cos, sin, pool_mix, pool_scale, k_m, v_m, mem_q_gain):
    b, s, d = x3.shape
    width = MOBA_HEADS * HEAD_DIM
    pool_width = POOL_GROUP * len(POOL_WINDOWS)
    mem_width = MEM_HEADS * MEM_HEAD_DIM
    m_len = k_m.shape[1]
    tm = min(TM_PROJ, s)
    nt = s // tm
    o0 = 6 * width
    w_qkv = w_in[:, :o0].astype(BF16)
    w_f = jnp.pad(w_in[:, o0:o0 + FOX_HEADS], ((0, 0), (0, LANES - FOX_HEADS))).astype(BF16)
    b_f = jnp.pad(b_forget, (0, LANES - FOX_HEADS))[None, :]
    o1 = o0 + FOX_HEADS
    w_pool = w_in[:, o1:o1 + pool_width].astype(BF16)
    o2 = o1 + pool_width
    w_mq = w_in[:, o2:o2 + mem_width].astype(BF16)
    gains = jnp.stack([jnp.tile(g, HEADS_PER_GROUP)
                       for g in (moba_q_gain, moba_k_gain, fox_q_gain, fox_k_gain)])

    tile = lambda w: pl.BlockSpec((1, tm, w), lambda i, j: (i, j, 0))
    act = lambda w: jax.ShapeDtypeStruct((b, s, w), BF16)
    outs = pl.pallas_call(
        _proj_kernel,
        out_shape=(act(width),) * 6 + (jax.ShapeDtypeStruct((b, s, LANES), F32), act(pool_width),
                                       act(mem_width), jax.ShapeDtypeStruct((b, nt, SUBLANES, width), F32)),
        grid=(b, nt),
        in_specs=[tile(d), _resident((1, d)), _resident((d, o0)), _resident((d, LANES)),
                  _resident((d, pool_width)), _resident((d, mem_width)), _resident((1, LANES)),
                  _resident((4, LANES)), tile(LANES), tile(LANES),
                  _resident((len(POOL_WINDOWS), POOL_GROUP, POOL_GROUP)), _resident((1, pool_width)),
                  pl.BlockSpec((1, m_len, mem_width), lambda i, j: (i, 0, 0)),
                  pl.BlockSpec((1, m_len, mem_width), lambda i, j: (i, 0, 0)),
                  _resident((1, MEM_HEAD_DIM))],
        out_specs=(tile(width),) * 6 + (tile(LANES), tile(pool_width), tile(mem_width),
                                        pl.BlockSpec((1, 1, SUBLANES, width), lambda i, j: (i, j, 0, 0))),
        scratch_shapes=[pltpu.VMEM((POOL_HALO, pool_width), F32), pltpu.VMEM((1, LANES), F32)],
        compiler_params=pltpu.CompilerParams(dimension_semantics=("arbitrary", "arbitrary"),
                                             vmem_limit_bytes=VMEM_LIMIT),
        name="mixer_projection",
    )(x3, norm_gain[None, :], w_qkv, w_f, w_pool, w_mq, b_f, gains, cos, sin,
      pool_mix.astype(BF16), pool_scale[None, :], k_m, v_m, mem_q_gain[None, :])
    qa, ka, va, qc, kc, vc, c, ob, om, kmean = outs
    kmean = kmean[:, :, :tm // MOBA_BLOCK, :].reshape(b, s // MOBA_BLOCK, width)
    return qa, ka, va, qc, kc, vc, c, ob, om, kmean


def _stage_values(v_ref, vt_ref, r0, rows):
    vt = v_ref[0, pl.ds(r0, rows), :].T
    row = lax.broadcasted_iota(jnp.int32, (AUG_ROWS, rows), 0)
    tail = jnp.where(row == 0, 1.0, 0.0).astype(BF16)
    for hh in range(HEADS_PER_GROUP):
        vt_ref[hh, :, pl.ds(r0, rows)] = jnp.concatenate([vt[hh * HEAD_DIM:(hh + 1) * HEAD_DIM], tail], axis=0)


def _stage_keys(k_ref, kaug_ref, r0, rows, ext0, ext1):
    k = k_ref[0, pl.ds(r0, rows), :]
    lane = lax.broadcasted_iota(jnp.int32, k.shape, 1)
    kaug_ref[0, pl.ds(r0, rows), :] = jnp.where(lane < HEAD_DIM, k, ext0.astype(BF16))
    kaug_ref[1, pl.ds(r0, rows), :] = jnp.where(lane >= HEAD_DIM, k, ext1.astype(BF16))


def _augment_queries(qt, e0, e1):
    tq = qt.shape[1]
    pad = jnp.zeros((HEAD_DIM - AUG_ROWS, tq), F32)
    qa0 = jnp.concatenate([qt[:HEAD_DIM], e0, pad], axis=0).astype(BF16)
    qa1 = jnp.concatenate([e1, pad, qt[HEAD_DIM:]], axis=0).astype(BF16)
    return qa0, qa1


def _issue_scores(qa, kaug_ref, s_ref, jb, slot):
    k0 = pl.multiple_of(jb * TK_ATTN, TK_ATTN)
    maxima = []
    for hh in range(HEADS_PER_GROUP):
        s = jnp.dot(kaug_ref[hh, pl.ds(k0, TK_ATTN), :], qa[hh], preferred_element_type=F32)
        s_ref[slot, hh] = s
        maxima.append(jnp.max(s, axis=0, keepdims=True))
    return tuple(maxima)


def _attend_pair(qa, mx_first, next_queries, kaug_ref, vt_ref, s_ref, i, o_ref):
    tq = o_ref.shape[0]
    tk = TK_ATTN
    key_i = lax.broadcasted_iota(jnp.int32, (tk, tq), 0)
    qry_i = lax.broadcasted_iota(jnp.int32, (tk, tq), 1)

    def issue_scores(jb, slot):
        return _issue_scores(qa, kaug_ref, s_ref, jb, slot)

    def causal_max(slot):
        return tuple(jnp.max(jnp.where(key_i <= qry_i, s_ref[slot, hh], NEG_INF), axis=0, keepdims=True)
                     for hh in range(HEADS_PER_GROUP))

    def accumulate(jb, slot, carry, mx, causal=False):
        k0 = pl.multiple_of(jb * tk, tk)
        new = []
        for hh in range(HEADS_PER_GROUP):
            m, acc = carry[hh]
            s = s_ref[slot, hh]
            if causal:
                s = jnp.where(key_i <= qry_i, s, NEG_INF)
            m_new = jnp.maximum(m, mx[hh])
            alpha = jnp.exp2(m - m_new)
            p = jnp.exp2(s - m_new).astype(BF16)
            vb = vt_ref[hh, :, pl.ds(k0, tk)]
            new.append((m_new, alpha * acc + jnp.dot(vb, p, preferred_element_type=F32)))
        return tuple(new)

    def step(t, state):
        carry, mx_even = state
        mx_odd = issue_scores(2 * t + 1, 1)
        carry = accumulate(2 * t, 0, carry, mx_even)
        mx_even = issue_scores(2 * t + 2, 0)
        carry = accumulate(2 * t + 1, 1, carry, mx_odd)
        return carry, mx_even

    init = tuple((jnp.full((1, tq), NEG_INF, F32), jnp.zeros((V_ROWS, tq), F32))
                 for _ in range(HEADS_PER_GROUP))
    def steps(t0, n, st):
        for d in range(n):
            st = step(t0 + d, st)
        return st

    state = lax.fori_loop(0, i // 4, lambda u, st: steps(4 * u, 4, st), (init, mx_first))
    done = (i // 4) * 4
    state = lax.cond(i - done >= 2, lambda st: steps(done, 2, st), lambda st: st, state)
    carry, _ = lax.cond(i % 2 == 1, lambda st: step(i - 1, st), lambda st: st, state)
    half = tq // 2
    k_last = pl.multiple_of((2 * i + 1) * tk, tk)
    for hh in range(HEADS_PER_GROUP):
        s_ref[1, hh, :, half:] = jnp.dot(kaug_ref[hh, pl.ds(k_last, tk), :], qa[hh][:, half:],
                                         preferred_element_type=F32)
    qa_next = next_queries()
    carry = accumulate(2 * i, 0, carry, causal_max(0), causal=True)
    mx_next = _issue_scores(qa_next, kaug_ref, s_ref, 0, 0)
    causal_half = (lax.broadcasted_iota(jnp.int32, (tk, half), 0)
                   <= lax.broadcasted_iota(jnp.int32, (tk, half), 1))
    accs = []
    for hh in range(HEADS_PER_GROUP):
        m, acc = carry[hh]
        s = jnp.where(causal_half, s_ref[1, hh, :, half:], NEG_INF)
        m_new = jnp.maximum(m[:, half:], jnp.max(s, axis=0, keepdims=True))
        alpha = jnp.exp2(m[:, half:] - m_new)
        p = jnp.exp2(s - m_new).astype(BF16)
        right = alpha * acc[:, half:] + jnp.dot(vt_ref[hh, :, pl.ds(k_last, tk)], p, preferred_element_type=F32)
        accs.append(jnp.concatenate([acc[:, :half], right], axis=1))
    ot = jnp.concatenate([acc[:HEAD_DIM] / acc[HEAD_DIM:HEAD_DIM + 1] for acc in accs], axis=0)
    o_ref[...] = ot.T.astype(o_ref.dtype)
    return qa_next, mx_next


def _split3(x):
    hi = x.astype(BF16).astype(F32)
    r = x - hi
    mid = r.astype(BF16).astype(F32)
    return hi, mid, r - mid


def _fox_kernel(q_ref, k_ref, v_ref, c_ref, crow_ref, o_ref, kaug_ref, vt_ref, s_ref):
    g = pl.program_id(1)
    tq = TQ_ATTN
    s_len = k_ref.shape[1]
    n_parts = 3

    r = lax.broadcasted_iota(jnp.int32, (n_parts * LANES, 2 * LANES), 0)
    col = lax.broadcasted_iota(jnp.int32, (n_parts * LANES, 2 * LANES), 1)
    hit = jnp.zeros(r.shape, jnp.bool_)
    for part in range(n_parts):
        hit = hit | (((r == part * LANES + HEADS_PER_GROUP * g) & (col == HEAD_DIM + n_parts + part))
                     | ((r == part * LANES + HEADS_PER_GROUP * g + 1) & (col == LANES + n_parts + part)))
    place = jnp.where(hit, -1.0, 0.0).astype(BF16)
    lane2 = lax.broadcasted_iota(jnp.int32, (1, 2 * LANES), 1)
    ones = jnp.where(((lane2 >= HEAD_DIM) & (lane2 < HEAD_DIM + n_parts))
                     | ((lane2 >= LANES) & (lane2 < LANES + n_parts)), 1.0, 0.0)
    rows = min(STAGE_ROWS, s_len)
    per_trip = 2 if s_len % (2 * rows) == 0 else 1

    def stage(ci, _):
        for sub in range(per_trip):
            r0 = pl.multiple_of((ci * per_trip + sub) * rows, rows)
            _stage_values(v_ref, vt_ref, r0, rows)
            parts = jnp.concatenate(
                [p.astype(BF16) for p in _split3(c_ref[0, pl.ds(r0, rows), :] * LOG2E)], axis=1)
            ext = ones + jnp.dot(parts, place, preferred_element_type=F32)
            _stage_keys(k_ref, kaug_ref, r0, rows, ext[:, :LANES], ext[:, LANES:])
        return 0

    lax.fori_loop(0, s_len // (rows * per_trip), stage, 0)

    row = lax.broadcasted_iota(jnp.int32, (AUG_ROWS, tq), 0)

    def queries(i):
        q0 = pl.multiple_of(i * tq, tq)
        qt = q_ref[0, pl.ds(q0, tq), :].astype(F32).T
        exts = []
        for hh in range(HEADS_PER_GROUP):
            hi, mid, lo = _split3(crow_ref[0, 0, hh:hh + 1, pl.ds(q0, tq)] * LOG2E)
            e = jnp.where(row == 0, hi, jnp.where(row == 1, mid, jnp.where(row == 2, lo, 0.0)))
            exts.append(jnp.where((row >= n_parts) & (row < 2 * n_parts), 1.0, e))
        return _augment_queries(qt, *exts)

    _run_query_tiles(queries, kaug_ref, vt_ref, s_ref, o_ref, s_len // tq)


def _run_query_tiles(queries, kaug_ref, vt_ref, s_ref, o_ref, n_tiles):
    tq = TQ_ATTN

    def tile(i, state):
        qa, mx_first = state
        q0 = pl.multiple_of(i * tq, tq)
        nxt = lambda: queries(jnp.minimum(i + 1, n_tiles - 1))
        return _attend_pair(qa, mx_first, nxt, kaug_ref, vt_ref, s_ref, i, o_ref.at[0, pl.ds(q0, tq), :])

    first = queries(0)
    lax.fori_loop(0, n_tiles, tile, (first, _issue_scores(first, kaug_ref, s_ref, 0, 0)))


def _attention_call(kernel_fn, name, q, k, v, extra_inputs, extra_specs):
    b, s, width = q.shape
    n_grp = width // LANES
    tq = TQ_ATTN
    assert tq == 2 * TK_ATTN and s % tq == 0
    seq = pl.BlockSpec((1, s, LANES), lambda bi, g: (bi, 0, g))
    return pl.pallas_call(
        kernel_fn,
        out_shape=jax.ShapeDtypeStruct((b, s, width), BF16),
        grid=(b, n_grp),
        in_specs=[seq, seq, seq] + extra_specs,
        out_specs=seq,
        scratch_shapes=[pltpu.VMEM((HEADS_PER_GROUP, s, LANES), BF16),
                        pltpu.VMEM((HEADS_PER_GROUP, V_ROWS, s), BF16),
                        pltpu.VMEM((2, HEADS_PER_GROUP, TK_ATTN, tq), F32)],
        compiler_params=_params(),
        name=name,
    )(q, k, v, *extra_inputs)


def _fox_attention(q, k, v, c):
    b, s, width = q.shape
    n_grp = width // LANES
    crow = c[:, :, :FOX_HEADS].transpose(0, 2, 1).reshape(b, n_grp, HEADS_PER_GROUP, s)
    return _attention_call(
        _fox_kernel, "forgetting_attention", q, k, v, (c, crow),
        [pl.BlockSpec((1, s, LANES), lambda bi, g: (bi, 0, 0)),
         pl.BlockSpec((1, 1, HEADS_PER_GROUP, s), lambda bi, g: (bi, g, 0, 0))])


def _moba_kernel(q_ref, k_ref, v_ref, kmean_ref, o_ref, kaug_ref, vt_ref, s_ref):
    tq = TQ_ATTN
    s_len = k_ref.shape[1]

    _stage_values(v_ref, vt_ref, 0, s_len)
    lane_k = lax.broadcasted_iota(jnp.int32, (MOBA_BLOCK, LANES), 1)
    lane_1 = lax.broadcasted_iota(jnp.int32, (1, LANES), 1)

    def stage_block(n, _):
        r0 = pl.multiple_of(n * MOBA_BLOCK, MOBA_BLOCK)
        k = k_ref[0, pl.ds(r0, MOBA_BLOCK), :]
        hot_hi = jnp.where(lane_1 == n + HEAD_DIM, 1.0, 0.0).astype(BF16)
        hot_lo = jnp.where(lane_1 == n, 1.0, 0.0).astype(BF16)
        kaug_ref[0, pl.ds(r0, MOBA_BLOCK), :] = jnp.where(lane_k < HEAD_DIM, k, hot_hi)
        kaug_ref[1, pl.ds(r0, MOBA_BLOCK), :] = jnp.where(lane_k >= HEAD_DIM, k, hot_lo)
        return 0

    lax.fori_loop(0, s_len // MOBA_BLOCK, stage_block, 0)

    kmean = kmean_ref[0]
    km_hi = kmean.astype(BF16)
    km_lo = (kmean - km_hi.astype(F32)).astype(BF16)
    zeros = jnp.zeros((HEAD_DIM, tq), F32)
    row = lax.broadcasted_iota(jnp.int32, (AUG_ROWS, tq), 0)
    row_f = row.astype(F32)
    qpos = lax.broadcasted_iota(jnp.int32, (AUG_ROWS, tq), 1)

    def queries(i):
        q0 = pl.multiple_of(i * tq, tq)
        qt = q_ref[0, pl.ds(q0, tq), :].astype(F32).T
        own = (q0 + qpos) // MOBA_BLOCK
        past = row < own
        exts = []
        for hh in range(HEADS_PER_GROUP):
            qz = (jnp.concatenate([qt[:HEAD_DIM], zeros], axis=0) if hh == 0
                  else jnp.concatenate([zeros, qt[HEAD_DIM:]], axis=0)).astype(BF16)
            gate = (jnp.dot(km_hi, qz, preferred_element_type=F32)
                    + jnp.dot(km_lo, qz, preferred_element_type=F32))
            gate = jnp.where(past, gate, NEG_INF)
            keep = row == own
            for _ in range(MOBA_TOPK):
                top = jnp.max(gate, axis=0, keepdims=True)
                idx = jnp.min(jnp.where(gate == top, row_f, float(AUG_ROWS)), axis=0, keepdims=True)
                pick = row_f == idx
                keep = keep | (pick & past)
                gate = jnp.where(pick, -jnp.inf, gate)
            exts.append(jnp.where(keep, 0.0, NEG_INF))
        return _augment_queries(qt, *exts)

    _run_query_tiles(queries, kaug_ref, vt_ref, s_ref, o_ref, s_len // tq)


def _moba_attention(q, k, v, kmean):
    b, s, width = q.shape
    n_blk = s // MOBA_BLOCK
    assert n_blk <= AUG_ROWS and TK_ATTN == MOBA_BLOCK
    kmean = jnp.pad(kmean, ((0, 0), (0, AUG_ROWS - n_blk), (0, 0)))
    return _attention_call(
        _moba_kernel, "moba_attention", q, k, v, (kmean,),
        [pl.BlockSpec((1, AUG_ROWS, LANES), lambda bi, g: (bi, 0, g))])


def _merge_kernel(x_ref, gn_ref, oa_ref, ob_ref, oc_ref, om_ref, wg_ref, wbr_ref, wo_ref, o_ref):
    x = x_ref[...]
    d = x.shape[1]
    h = _rms(x, gn_ref[...]).astype(BF16)
    mix = jnp.zeros_like(x)
    for n, br in enumerate((oa_ref, ob_ref, oc_ref, om_ref)):
        gate = jax.nn.sigmoid(jnp.dot(h, wg_ref[:, n * d:(n + 1) * d], preferred_element_type=F32))
        y = jnp.dot(br[...], wbr_ref[n], preferred_element_type=F32)
        mix = mix + gate * y
    o_ref[...] = x + jnp.dot(mix.astype(BF16), wo_ref[...], preferred_element_type=F32)


def _merge(x2, norm_gain, branches, w_gate, w_branch, w_out):
    t, d = x2.shape
    bw = w_branch.shape[1]
    tm = min(TM_FFN, t)
    row = lambda w: pl.BlockSpec((tm, w), lambda i: (i, 0))
    return pl.pallas_call(
        _merge_kernel,
        out_shape=jax.ShapeDtypeStruct((t, d), F32),
        grid=(t // tm,),
        in_specs=[row(d), _resident((1, d)), row(bw), row(bw), row(bw), row(bw),
                  _resident((d, N_BRANCH * d)), _resident((N_BRANCH, bw, d)), _resident((d, d))],
        out_specs=row(d),
        compiler_params=_params(),
        name="gated_merge",
    )(x2, norm_gain[None, :], *branches, w_gate.astype(BF16), w_branch.astype(BF16), w_out.astype(BF16))


def kernel(x, mem, positions, norm_ffn1, w_ffn1_in, w_ffn1_out, norm_mix, w_in, b_forget, moba_q_gain, moba_k_gain, fox_q_gain, fox_k_gain, norm_mem, w_mem_kv, mem_q_gain, mem_k_gain, pool_mix, pool_scale, w_branch, w_out, norm_ffn2, w_ffn2_in, w_ffn2_out):
    b, s, d = x.shape
    depth = norm_ffn1.shape[0]
    gate_off = w_in.shape[2] - N_BRANCH * d
    cos, sin = _rope_tables(positions)
    x2 = x.reshape(b * s, d)
    for l in range(depth):
        x2 = _ffn(x2, norm_ffn1[l], w_ffn1_in[l], w_ffn1_out[l])
        k_m, v_m = _mem_kv(mem, norm_mem[l], w_mem_kv[l], mem_k_gain[l])
        qa, ka, va, qc, kc, vc, c, ob, om, kmean = _project(
            x2.reshape(b, s, d), norm_mix[l], w_in[l, :, :gate_off], b_forget[l],
            moba_q_gain[l], moba_k_gain[l], fox_q_gain[l], fox_k_gain[l],
            cos, sin, pool_mix[l], pool_scale[l], k_m, v_m, mem_q_gain[l])
        oa = _moba_attention(qa, ka, va, kmean)
        oc = _fox_attention(qc, kc, vc, c)
        flat = lambda a: a.reshape(b * s, a.shape[2])
        x2 = _merge(x2, norm_mix[l], (flat(oa), flat(ob), flat(oc), flat(om)),
                    w_in[l, :, gate_off:], w_branch[l], w_out[l])
        x2 = _ffn(x2, norm_ffn2[l], w_ffn2_in[l], w_ffn2_out[l])
    return x2.reshape(b, s, d)
```

```python
import jax
import jax.numpy as jnp
from jax import lax
from jax.experimental import pallas as pl
from jax.experimental.pallas import tpu as pltpu

HEAD_DIM = 64
MOBA_HEADS = 8
MOBA_BLOCK = 256
MOBA_TOPK = 3
FOX_HEADS = 8
POOL_WINDOWS = (2, 4, 8, 16)
POOL_GROUP = 128
MEM_HEADS = 4
MEM_HEAD_DIM = 128
ROPE_THETA = 500000.0
ROT_DIM = HEAD_DIM // 4
N_BRANCH = 4
RMS_EPS = 1e-6
NEG_INF = -1e30

LANES = 128
SUBLANES = 8
HEADS_PER_GROUP = LANES // HEAD_DIM
POOL_HALO = 16
VMEM_LIMIT = 56 * 1024 * 1024

TM_FFN = 1024
TM_PROJ = 1024
TS_ROPE = 1024
TQ_ATTN = 512
TK_ATTN = 256
STAGE_ROWS = 512
AUG_ROWS = 16
FFN_CHUNK = 256

LOG2E = 1.4426950408889634
V_ROWS = HEAD_DIM + AUG_ROWS

F32 = jnp.float32
BF16 = jnp.bfloat16
_NT = (((1,), (1,)), ((), ()))


def _params():
    return pltpu.CompilerParams(vmem_limit_bytes=VMEM_LIMIT)


def _resident(shape):
    nd = len(shape)
    return pl.BlockSpec(shape, lambda *_: (0,) * nd, pipeline_mode=pl.Buffered(1))


def _rms(x, gain):
    ms = jnp.mean(x * x, axis=-1, keepdims=True)
    return x * lax.rsqrt(ms + RMS_EPS) * gain


def _head_pair_rms(x, gain):
    lane = lax.broadcasted_iota(jnp.int32, x.shape, 1)
    lo = lane < HEAD_DIM
    sq = x * x
    s_lo = jnp.sum(jnp.where(lo, sq, 0.0), axis=-1, keepdims=True)
    s_hi = jnp.sum(jnp.where(lo, 0.0, sq), axis=-1, keepdims=True)
    ms = jnp.where(lo, s_lo, s_hi) * (1.0 / HEAD_DIM)
    return x * lax.rsqrt(ms + RMS_EPS) * gain


def _rotate(x, cos, sin):
    lane = lax.broadcasted_iota(jnp.int32, x.shape, 1)
    half = ROT_DIM // 2
    fwd = pltpu.roll(x, LANES - half, axis=1)
    bwd = pltpu.roll(x, half, axis=1)
    partner = jnp.where((lane % HEAD_DIM) < half, fwd, bwd)
    return x * cos + partner * sin


def _rope_kernel(pos_ref, freq_ref, sign_ref, cos_ref, sin_ref):
    ang = freq_ref[...] * pos_ref[0].astype(F32)
    cos_r = jnp.cos(ang)
    sin_r = jnp.sin(ang) * sign_ref[...]
    ts = ang.shape[1]
    one = jnp.ones((HEAD_DIM - ROT_DIM, ts), F32)
    zero = jnp.zeros((HEAD_DIM - ROT_DIM, ts), F32)
    cos_ref[0] = jnp.concatenate([cos_r, one] * HEADS_PER_GROUP, axis=0).T
    sin_ref[0] = jnp.concatenate([sin_r, zero] * HEADS_PER_GROUP, axis=0).T


def _rope_tables(positions):
    b, s = positions.shape
    half = ROT_DIM // 2
    inv_freq = jnp.power(ROPE_THETA, -jnp.arange(half, dtype=F32) * (2.0 / ROT_DIM))
    freq = jnp.tile(inv_freq, 2)[:, None]
    sign = jnp.concatenate([-jnp.ones((half,), F32), jnp.ones((half,), F32)])[:, None]
    ts = min(s, TS_ROPE)
    return pl.pallas_call(
        _rope_kernel,
        out_shape=(jax.ShapeDtypeStruct((b, s, LANES), F32),) * 2,
        grid=(b, s // ts),
        in_specs=[pl.BlockSpec((1, 1, ts), lambda i, j: (i, 0, j)),
                  pl.BlockSpec((ROT_DIM, 1), lambda i, j: (0, 0)),
                  pl.BlockSpec((ROT_DIM, 1), lambda i, j: (0, 0))],
        out_specs=(pl.BlockSpec((1, ts, LANES), lambda i, j: (i, j, 0)),) * 2,
        compiler_params=_params(),
        name="rope_tables",
    )(positions.reshape(b, 1, s), freq, sign)


def _ffn_kernel(x_ref, g_ref, wg_ref, wu_ref, wo_ref, o_ref):
    x = x_ref[...]
    h = _rms(x, g_ref[...]).astype(BF16)
    d_ff = wg_ref.shape[1]
    acc = jnp.zeros_like(x)
    for c in range(d_ff // FFN_CHUNK):
        sl = slice(c * FFN_CHUNK, (c + 1) * FFN_CHUNK)
        g = jnp.dot(h, wg_ref[:, sl], preferred_element_type=F32)
        u = jnp.dot(h, wu_ref[:, sl], preferred_element_type=F32)
        a = (g * jax.nn.sigmoid(g) * u).astype(BF16)
        acc = acc + jnp.dot(a, wo_ref[sl, :], preferred_element_type=F32)
    o_ref[...] = x + 0.5 * acc


def _ffn(x2, gain, w_in, w_out):
    t, d = x2.shape
    d_ff = w_out.shape[0]
    wg = w_in[:, :d_ff].astype(BF16)
    wu = w_in[:, d_ff:].astype(BF16)
    wo = w_out.astype(BF16)
    tm = min(TM_FFN, t)
    return pl.pallas_call(
        _ffn_kernel,
        out_shape=jax.ShapeDtypeStruct((t, d), F32),
        grid=(t // tm,),
        in_specs=[pl.BlockSpec((tm, d), lambda i: (i, 0)),
                  _resident((1, d)), _resident((d, d_ff)), _resident((d, d_ff)), _resident((d_ff, d))],
        out_specs=pl.BlockSpec((tm, d), lambda i: (i, 0)),
        compiler_params=_params(),
        name="ffn_half_step",
    )(x2, gain[None, :], wg, wu, wo)


def _memkv_kernel(mem_ref, gn_ref, w_ref, kg_ref, k_ref, v_ref):
    width = k_ref.shape[2]
    mn = _rms(mem_ref[0], gn_ref[...]).astype(BF16)
    kv = jnp.dot(mn, w_ref[...], preferred_element_type=F32)
    for h in range(MEM_HEADS):
        sl = slice(h * MEM_HEAD_DIM, (h + 1) * MEM_HEAD_DIM)
        k_ref[0, :, sl] = _rms(kv[:, sl], kg_ref[...]).astype(BF16)
    v_ref[0] = kv[:, width:].astype(BF16)


def _mem_kv(mem, norm_gain, w_kv, k_gain):
    b, m, d = mem.shape
    width = MEM_HEADS * MEM_HEAD_DIM
    return pl.pallas_call(
        _memkv_kernel,
        out_shape=(jax.ShapeDtypeStruct((b, m, width), BF16),) * 2,
        grid=(b,),
        in_specs=[pl.BlockSpec((1, m, d), lambda i: (i, 0, 0)),
                  _resident((1, d)), _resident((d, 2 * width)), _resident((1, MEM_HEAD_DIM))],
        out_specs=(pl.BlockSpec((1, m, width), lambda i: (i, 0, 0)),) * 2,
        compiler_params=_params(),
        name="memory_kv",
    )(mem, norm_gain[None, :], w_kv.astype(BF16), k_gain[None, :])


def _proj_kernel(x_ref, gn_ref, wqkv_ref, wf_ref, wp_ref, wmq_ref, bf_ref, gains_ref,
                 cos_ref, sin_ref, pmix_ref, pscale_ref, km_ref, vm_ref, mqg_ref,
                 qa_ref, ka_ref, va_ref, qc_ref, kc_ref, vc_ref, c_ref, ob_ref, om_ref, kmean_ref,
                 pool_carry, c_carry):
    j = pl.program_id(1)
    tm = x_ref.shape[1]
    width = qa_ref.shape[2]
    n_grp = width // LANES

    @pl.when(j == 0)
    def _():
        pool_carry[...] = jnp.zeros_like(pool_carry)
        c_carry[...] = jnp.zeros_like(c_carry)

    h = _rms(x_ref[0], gn_ref[...]).astype(BF16)
    scale = HEAD_DIM ** -0.5 * LOG2E
    cos = cos_ref[0]
    sin = sin_ref[0]
    kmean_ref[...] = jnp.zeros_like(kmean_ref)

    def project(w_ref, lo=None, hi=None):
        return lambda: jnp.dot(h, w_ref[...] if lo is None else w_ref[:, lo:hi], preferred_element_type=F32)

    def finish_heads(out_ref, gain_row=None, rot=False, mul=None, means=False):
        def run(y):
            for g in range(n_grp):
                sl = slice(g * LANES, (g + 1) * LANES)
                t = y[:, sl]
                if gain_row is not None:
                    t = _head_pair_rms(t, gains_ref[gain_row:gain_row + 1, :])
                if rot:
                    t = _rotate(t, cos, sin)
                if means:
                    for sb in range(tm // MOBA_BLOCK):
                        kmean_ref[0, 0, sb:sb + 1, sl] = jnp.mean(
                            t[sb * MOBA_BLOCK:(sb + 1) * MOBA_BLOCK], axis=0, keepdims=True)
                if mul is not None:
                    t = t * mul
                out_ref[0, :, sl] = t.astype(BF16)
        return run

    def finish_forget(y):
        lf = jax.nn.log_sigmoid(y + bf_ref[...])
        row = lax.broadcasted_iota(jnp.int32, lf.shape, 0)
        step = 1
        while step < tm:
            lf = lf + jnp.where(row >= step, pltpu.roll(lf, step, axis=0), 0.0)
            step *= 2
        c = lf + c_carry[...]
        c_ref[0] = c
        c_carry[...] = c[tm - 1:tm, :]

    def finish_pool(u):
        ext = jnp.concatenate([pool_carry[...], u], axis=0)
        pos = j * tm + lax.broadcasted_iota(jnp.int32, (tm, 1), 0)
        pooled = []
        for g, w in enumerate(POOL_WINDOWS):
            sl = slice(g * POOL_GROUP, (g + 1) * POOL_GROUP)
            win = ext[:, sl]
            step = 1
            while step < w:
                win = win + pltpu.roll(win, step, axis=0)
                step *= 2
            cnt = jnp.minimum(pos + 1, w).astype(F32)
            pooled.append((win[POOL_HALO:, :] / cnt - u[:, sl]).astype(BF16))
        for g in range(len(POOL_WINDOWS)):
            sl = slice(g * POOL_GROUP, (g + 1) * POOL_GROUP)
            mixed = jnp.dot(pooled[g], pmix_ref[g], preferred_element_type=F32)
            ob_ref[0, :, sl] = (mixed * pscale_ref[:, sl]).astype(BF16)
        pool_carry[...] = u[tm - POOL_HALO:, :]

    def finish_memory(qm_all):
        heads = [slice(hd * MEM_HEAD_DIM, (hd + 1) * MEM_HEAD_DIM) for hd in range(MEM_HEADS)]
        scores = [lax.dot_general(_rms(qm_all[:, sl], mqg_ref[...]).astype(BF16), km_ref[0, :, sl], _NT,
                                  preferred_element_type=F32) * (MEM_HEAD_DIM ** -0.5) for sl in heads]
        probs = [jnp.exp(s - jnp.max(s, axis=-1, keepdims=True)) for s in scores]
        for sl, p in zip(heads, probs):
            den = jnp.sum(p, axis=-1, keepdims=True)
            o = jnp.dot(p.astype(BF16), vm_ref[0, :, sl], preferred_element_type=F32)
            om_ref[0, :, sl] = (o / den).astype(BF16)

    stages = [
        (project(wqkv_ref, 0 * width, 1 * width), finish_heads(qa_ref, 0, rot=True, mul=scale)),
        (project(wp_ref), finish_pool),
        (project(wqkv_ref, 1 * width, 2 * width), finish_heads(ka_ref, 1, rot=True, means=True)),
        (project(wf_ref), finish_forget),
        (project(wqkv_ref, 2 * width, 3 * width), finish_heads(va_ref)),
        (project(wqkv_ref, 3 * width, 4 * width), finish_heads(qc_ref, 2, mul=scale)),
        (project(wmq_ref), finish_memory),
        (project(wqkv_ref, 4 * width, 5 * width), finish_heads(kc_ref, 3)),
        (project(wqkv_ref, 5 * width, 6 * width), finish_heads(vc_ref)),
    ]
    pending = None
    for matmul, finish in stages:
        y = matmul()
        if pending is not None:
            pending[0](pending[1])
        pending = (finish, y)
    pending[0](pending[1])


def _project(x3, norm_gain, w_in, b_forget, moba_q_gain, moba_k_gain, fox_q_gain, fox_k_gain,
             cos, sin, pool_mix, pool_scale, k_m, v_m, mem_q_gain):
    b, s, d = x3.shape
    width = MOBA_HEADS * HEAD_DIM
    pool_width = POOL_GROUP * len(POOL_WINDOWS)
    mem_width = MEM_HEADS * MEM_HEAD_DIM
    m_len = k_m.shape[1]
    tm = min(TM_PROJ, s)
    nt = s // tm
    o0 = 6 * width
    w_qkv = w_in[:, :o0].astype(BF16)
    w_f = jnp.pad(w_in[:, o0:o0 + FOX_HEADS], ((0, 0), (0, LANES - FOX_HEADS))).astype(BF16)
    b_f = jnp.pad(b_forget, (0, LANES - FOX_HEADS))[None, :]
    o1 = o0 + FOX_HEADS
    w_pool = w_in[:, o1:o1 + pool_width].astype(BF16)
    o2 = o1 + pool_width
    w_mq = w_in[:, o2:o2 + mem_width].astype(BF16)
    gains = jnp.stack([jnp.tile(g, HEADS_PER_GROUP)
                       for g in (moba_q_gain, moba_k_gain, fox_q_gain, fox_k_gain)])

    tile = lambda w: pl.BlockSpec((1, tm, w), lambda i, j: (i, j, 0))
    act = lambda w: jax.ShapeDtypeStruct((b, s, w), BF16)
    outs = pl.pallas_call(
        _proj_kernel,
        out_shape=(act(width),) * 6 + (jax.ShapeDtypeStruct((b, s, LANES), F32), act(pool_width),
                                       act(mem_width), jax.ShapeDtypeStruct((b, nt, SUBLANES, width), F32)),
        grid=(b, nt),
        in_specs=[tile(d), _resident((1, d)), _resident((d, o0)), _resident((d, LANES)),
                  _resident((d, pool_width)), _resident((d, mem_width)), _resident((1, LANES)),
                  _resident((4, LANES)), tile(LANES), tile(LANES),
                  _resident((len(POOL_WINDOWS), POOL_GROUP, POOL_GROUP)), _resident((1, pool_width)),
                  pl.BlockSpec((1, m_len, mem_width), lambda i, j: (i, 0, 0)),
                  pl.BlockSpec((1, m_len, mem_width), lambda i, j: (i, 0, 0)),
                  _resident((1, MEM_HEAD_DIM))],
        out_specs=(tile(width),) * 6 + (tile(LANES), tile(pool_width), tile(mem_width),
                                        pl.BlockSpec((1, 1, SUBLANES, width), lambda i, j: (i, j, 0, 0))),
        scratch_shapes=[pltpu.VMEM((POOL_HALO, pool_width), F32), pltpu.VMEM((1, LANES), F32)],
        compiler_params=pltpu.CompilerParams(dimension_semantics=("arbitrary", "arbitrary"),
                                             vmem_limit_bytes=VMEM_LIMIT),
        name="mixer_projection",
    )(x3, norm_gain[None, :], w_qkv, w_f, w_pool, w_mq, b_f, gains, cos, sin,
      pool_mix.astype(BF16), pool_scale[None, :], k_m, v_m, mem_q_gain[None, :])
    qa, ka, va, qc, kc, vc, c, ob, om, kmean = outs
    kmean = kmean[:, :, :tm // MOBA_BLOCK, :].reshape(b, s // MOBA_BLOCK, width)
    return qa, ka, va, qc, kc, vc, c, ob, om, kmean


def _stage_values(v_ref, vt_ref, r0, rows):
    vt = v_ref[0, pl.ds(r0, rows), :].T
    row = lax.broadcasted_iota(jnp.int32, (AUG_ROWS, rows), 0)
    tail = jnp.where(row == 0, 1.0, 0.0).astype(BF16)
    for hh in range(HEADS_PER_GROUP):
        vt_ref[hh, :, pl.ds(r0, rows)] = jnp.concatenate([vt[hh * HEAD_DIM:(hh + 1) * HEAD_DIM], tail], axis=0)


def _stage_keys(k_ref, kaug_ref, r0, rows, ext0, ext1):
    k = k_ref[0, pl.ds(r0, rows), :]
    lane = lax.broadcasted_iota(jnp.int32, k.shape, 1)
    kaug_ref[0, pl.ds(r0, rows), :] = jnp.where(lane < HEAD_DIM, k, ext0.astype(BF16))
    kaug_ref[1, pl.ds(r0, rows), :] = jnp.where(lane >= HEAD_DIM, k, ext1.astype(BF16))


def _augment_queries(qt, e0, e1):
    tq = qt.shape[1]
    pad = jnp.zeros((HEAD_DIM - AUG_ROWS, tq), F32)
    qa0 = jnp.concatenate([qt[:HEAD_DIM], e0, pad], axis=0).astype(BF16)
    qa1 = jnp.concatenate([e1, pad, qt[HEAD_DIM:]], axis=0).astype(BF16)
    return qa0, qa1


def _issue_scores(qa, kaug_ref, s_ref, jb, slot):
    k0 = pl.multiple_of(jb * TK_ATTN, TK_ATTN)
    maxima = []
    for hh in range(HEADS_PER_GROUP):
        s = jnp.dot(kaug_ref[hh, pl.ds(k0, TK_ATTN), :], qa[hh], preferred_element_type=F32)
        s_ref[slot, hh] = s
        maxima.append(jnp.max(s, axis=0, keepdims=True))
    return tuple(maxima)


def _attend_pair(qa, mx_first, next_queries, kaug_ref, vt_ref, s_ref, i, o_ref):
    tq = o_ref.shape[0]
    tk = TK_ATTN
    key_i = lax.broadcasted_iota(jnp.int32, (tk, tq), 0)
    qry_i = lax.broadcasted_iota(jnp.int32, (tk, tq), 1)

    def issue_scores(jb, slot):
        return _issue_scores(qa, kaug_ref, s_ref, jb, slot)

    def causal_max(slot):
        return tuple(jnp.max(jnp.where(key_i <= qry_i, s_ref[slot, hh], NEG_INF), axis=0, keepdims=True)
                     for hh in range(HEADS_PER_GROUP))

    def accumulate(jb, slot, carry, mx, causal=False):
        k0 = pl.multiple_of(jb * tk, tk)
        new = []
        for hh in range(HEADS_PER_GROUP):
            m, acc = carry[hh]
            s = s_ref[slot, hh]
            if causal:
                s = jnp.where(key_i <= qry_i, s, NEG_INF)
            m_new = jnp.maximum(m, mx[hh])
            alpha = jnp.exp2(m - m_new)
            p = jnp.exp2(s - m_new).astype(BF16)
            vb = vt_ref[hh, :, pl.ds(k0, tk)]
            new.append((m_new, alpha * acc + jnp.dot(vb, p, preferred_element_type=F32)))
        return tuple(new)

    def step(t, state):
        carry, mx_even = state
        mx_odd = issue_scores(2 * t + 1, 1)
        carry = accumulate(2 * t, 0, carry, mx_even)
        mx_even = issue_scores(2 * t + 2, 0)
        carry = accumulate(2 * t + 1, 1, carry, mx_odd)
        return carry, mx_even

    init = tuple((jnp.full((1, tq), NEG_INF, F32), jnp.zeros((V_ROWS, tq), F32))
                 for _ in range(HEADS_PER_GROUP))
    def steps(t0, n, st):
        for d in range(n):
            st = step(t0 + d, st)
        return st

    state = lax.fori_loop(0, i // 4, lambda u, st: steps(4 * u, 4, st), (init, mx_first))
    done = (i // 4) * 4
    state = lax.cond(i - done >= 2, lambda st: steps(done, 2, st), lambda st: st, state)
    def finish(st):
        carry, _ = st
        half = tq // 2
        k_last = pl.multiple_of((2 * i + 1) * tk, tk)
        for hh in range(HEADS_PER_GROUP):
            s_ref[1, hh, :, half:] = jnp.dot(kaug_ref[hh, pl.ds(k_last, tk), :], qa[hh][:, half:],
                                             preferred_element_type=F32)
        qa_next = next_queries()
        carry = accumulate(2 * i, 0, carry, causal_max(0), causal=True)
        mx_next = _issue_scores(qa_next, kaug_ref, s_ref, 0, 0)
        causal_half = (lax.broadcasted_iota(jnp.int32, (tk, half), 0)
                       <= lax.broadcasted_iota(jnp.int32, (tk, half), 1))
        accs = []
        for hh in range(HEADS_PER_GROUP):
            m, acc = carry[hh]
            s = jnp.where(causal_half, s_ref[1, hh, :, half:], NEG_INF)
            m_new = jnp.maximum(m[:, half:], jnp.max(s, axis=0, keepdims=True))
            alpha = jnp.exp2(m[:, half:] - m_new)
            p = jnp.exp2(s - m_new).astype(BF16)
            right = alpha * acc[:, half:] + jnp.dot(vt_ref[hh, :, pl.ds(k_last, tk)], p,
                                                    preferred_element_type=F32)
            accs.append(jnp.concatenate([acc[:, :half], right], axis=1))
        ot = jnp.concatenate([acc[:HEAD_DIM] / acc[HEAD_DIM:HEAD_DIM + 1] for acc in accs], axis=0)
        o_ref[...] = ot.T.astype(o_ref.dtype)
        return qa_next, mx_next

    return lax.cond(i % 2 == 1, lambda st: finish(step(i - 1, st)), finish, state)


def _split3(x):
    hi = x.astype(BF16).astype(F32)
    r = x - hi
    mid = r.astype(BF16).astype(F32)
    return hi, mid, r - mid


def _fox_kernel(q_ref, k_ref, v_ref, c_ref, crow_ref, o_ref, kaug_ref, vt_ref, s_ref):
    g = pl.program_id(1)
    tq = TQ_ATTN
    s_len = k_ref.shape[1]
    n_parts = 3

    r = lax.broadcasted_iota(jnp.int32, (n_parts * LANES, 2 * LANES), 0)
    col = lax.broadcasted_iota(jnp.int32, (n_parts * LANES, 2 * LANES), 1)
    hit = jnp.zeros(r.shape, jnp.bool_)
    for part in range(n_parts):
        hit = hit | (((r == part * LANES + HEADS_PER_GROUP * g) & (col == HEAD_DIM + n_parts + part))
                     | ((r == part * LANES + HEADS_PER_GROUP * g + 1) & (col == LANES + n_parts + part)))
    place = jnp.where(hit, -1.0, 0.0).astype(BF16)
    lane2 = lax.broadcasted_iota(jnp.int32, (1, 2 * LANES), 1)
    ones = jnp.where(((lane2 >= HEAD_DIM) & (lane2 < HEAD_DIM + n_parts))
                     | ((lane2 >= LANES) & (lane2 < LANES + n_parts)), 1.0, 0.0)
    rows = min(STAGE_ROWS, s_len)
    per_trip = 2 if s_len % (2 * rows) == 0 else 1

    def stage(ci, _):
        for sub in range(per_trip):
            r0 = pl.multiple_of((ci * per_trip + sub) * rows, rows)
            _stage_values(v_ref, vt_ref, r0, rows)
            parts = jnp.concatenate(
                [p.astype(BF16) for p in _split3(c_ref[0, pl.ds(r0, rows), :] * LOG2E)], axis=1)
            ext = ones + jnp.dot(parts, place, preferred_element_type=F32)
            _stage_keys(k_ref, kaug_ref, r0, rows, ext[:, :LANES], ext[:, LANES:])
        return 0

    lax.fori_loop(0, s_len // (rows * per_trip), stage, 0)

    row = lax.broadcasted_iota(jnp.int32, (AUG_ROWS, tq), 0)

    def queries(i):
        q0 = pl.multiple_of(i * tq, tq)
        qt = q_ref[0, pl.ds(q0, tq), :].astype(F32).T
        exts = []
        for hh in range(HEADS_PER_GROUP):
            hi, mid, lo = _split3(crow_ref[0, 0, hh:hh + 1, pl.ds(q0, tq)] * LOG2E)
            e = jnp.where(row == 0, hi, jnp.where(row == 1, mid, jnp.where(row == 2, lo, 0.0)))
            exts.append(jnp.where((row >= n_parts) & (row < 2 * n_parts), 1.0, e))
        return _augment_queries(qt, *exts)

    _run_query_tiles(queries, kaug_ref, vt_ref, s_ref, o_ref, s_len // tq)


def _run_query_tiles(queries, kaug_ref, vt_ref, s_ref, o_ref, n_tiles):
    tq = TQ_ATTN

    def tile(i, state):
        qa, mx_first = state
        q0 = pl.multiple_of(i * tq, tq)
        nxt = lambda: queries(jnp.minimum(i + 1, n_tiles - 1))
        return _attend_pair(qa, mx_first, nxt, kaug_ref, vt_ref, s_ref, i, o_ref.at[0, pl.ds(q0, tq), :])

    first = queries(0)
    lax.fori_loop(0, n_tiles, tile, (first, _issue_scores(first, kaug_ref, s_ref, 0, 0)))


def _attention_call(kernel_fn, name, q, k, v, extra_inputs, extra_specs):
    b, s, width = q.shape
    n_grp = width // LANES
    tq = TQ_ATTN
    assert tq == 2 * TK_ATTN and s % tq == 0
    seq = pl.BlockSpec((1, s, LANES), lambda bi, g: (bi, 0, g))
    return pl.pallas_call(
        kernel_fn,
        out_shape=jax.ShapeDtypeStruct((b, s, width), BF16),
        grid=(b, n_grp),
        in_specs=[seq, seq, seq] + extra_specs,
        out_specs=seq,
        scratch_shapes=[pltpu.VMEM((HEADS_PER_GROUP, s, LANES), BF16),
                        pltpu.VMEM((HEADS_PER_GROUP, V_ROWS, s), BF16),
                        pltpu.VMEM((2, HEADS_PER_GROUP, TK_ATTN, tq), F32)],
        compiler_params=_params(),
        name=name,
    )(q, k, v, *extra_inputs)


def _fox_attention(q, k, v, c):
    b, s, width = q.shape
    n_grp = width // LANES
    crow = c[:, :, :FOX_HEADS].transpose(0, 2, 1).reshape(b, n_grp, HEADS_PER_GROUP, s)
    return _attention_call(
        _fox_kernel, "forgetting_attention", q, k, v, (c, crow),
        [pl.BlockSpec((1, s, LANES), lambda bi, g: (bi, 0, 0)),
         pl.BlockSpec((1, 1, HEADS_PER_GROUP, s), lambda bi, g: (bi, g, 0, 0))])


def _moba_kernel(q_ref, k_ref, v_ref, kmean_ref, o_ref, kaug_ref, vt_ref, s_ref):
    tq = TQ_ATTN
    s_len = k_ref.shape[1]

    _stage_values(v_ref, vt_ref, 0, s_len)
    lane_k = lax.broadcasted_iota(jnp.int32, (MOBA_BLOCK, LANES), 1)
    lane_1 = lax.broadcasted_iota(jnp.int32, (1, LANES), 1)

    def stage_block(n, _):
        r0 = pl.multiple_of(n * MOBA_BLOCK, MOBA_BLOCK)
        k = k_ref[0, pl.ds(r0, MOBA_BLOCK), :]
        hot_hi = jnp.where(lane_1 == n + HEAD_DIM, 1.0, 0.0).astype(BF16)
        hot_lo = jnp.where(lane_1 == n, 1.0, 0.0).astype(BF16)
        kaug_ref[0, pl.ds(r0, MOBA_BLOCK), :] = jnp.where(lane_k < HEAD_DIM, k, hot_hi)
        kaug_ref[1, pl.ds(r0, MOBA_BLOCK), :] = jnp.where(lane_k >= HEAD_DIM, k, hot_lo)
        return 0

    lax.fori_loop(0, s_len // MOBA_BLOCK, stage_block, 0)

    kmean = kmean_ref[0]
    km_hi = kmean.astype(BF16)
    km_lo = (kmean - km_hi.astype(F32)).astype(BF16)
    zeros = jnp.zeros((HEAD_DIM, tq), F32)
    row = lax.broadcasted_iota(jnp.int32, (AUG_ROWS, tq), 0)
    row_f = row.astype(F32)
    qpos = lax.broadcasted_iota(jnp.int32, (AUG_ROWS, tq), 1)

    def queries(i):
        q0 = pl.multiple_of(i * tq, tq)
        qt = q_ref[0, pl.ds(q0, tq), :].astype(F32).T
        own = (q0 + qpos) // MOBA_BLOCK
        past = row < own
        exts = []
        for hh in range(HEADS_PER_GROUP):
            qz = (jnp.concatenate([qt[:HEAD_DIM], zeros], axis=0) if hh == 0
                  else jnp.concatenate([zeros, qt[HEAD_DIM:]], axis=0)).astype(BF16)
            gate = (jnp.dot(km_hi, qz, preferred_element_type=F32)
                    + jnp.dot(km_lo, qz, preferred_element_type=F32))
            gate = jnp.where(past, gate, NEG_INF)
            keep = row == own
            for _ in range(MOBA_TOPK):
                top = jnp.max(gate, axis=0, keepdims=True)
                idx = jnp.min(jnp.where(gate == top, row_f, float(AUG_ROWS)), axis=0, keepdims=True)
                pick = row_f == idx
                keep = keep | (pick & past)
                gate = jnp.where(pick, -jnp.inf, gate)
            exts.append(jnp.where(keep, 0.0, NEG_INF))
        return _augment_queries(qt, *exts)

    _run_query_tiles(queries, kaug_ref, vt_ref, s_ref, o_ref, s_len // tq)


def _moba_attention(q, k, v, kmean):
    b, s, width = q.shape
    n_blk = s // MOBA_BLOCK
    assert n_blk <= AUG_ROWS and TK_ATTN == MOBA_BLOCK
    kmean = jnp.pad(kmean, ((0, 0), (0, AUG_ROWS - n_blk), (0, 0)))
    return _attention_call(
        _moba_kernel, "moba_attention", q, k, v, (kmean,),
        [pl.BlockSpec((1, AUG_ROWS, LANES), lambda bi, g: (bi, 0, g))])


def _merge_kernel(x_ref, gn_ref, oa_ref, ob_ref, oc_ref, om_ref, wg_ref, wbr_ref, wo_ref, o_ref):
    x = x_ref[...]
    d = x.shape[1]
    h = _rms(x, gn_ref[...]).astype(BF16)
    mix = jnp.zeros_like(x)
    for n, br in enumerate((oa_ref, ob_ref, oc_ref, om_ref)):
        gate = jax.nn.sigmoid(jnp.dot(h, wg_ref[:, n * d:(n + 1) * d], preferred_element_type=F32))
        y = jnp.dot(br[...], wbr_ref[n], preferred_element_type=F32)
        mix = mix + gate * y
    o_ref[...] = x + jnp.dot(mix.astype(BF16), wo_ref[...], preferred_element_type=F32)


def _merge(x2, norm_gain, branches, w_gate, w_branch, w_out):
    t, d = x2.shape
    bw = w_branch.shape[1]
    tm = min(TM_FFN, t)
    row = lambda w: pl.BlockSpec((tm, w), lambda i: (i, 0))
    return pl.pallas_call(
        _merge_kernel,
        out_shape=jax.ShapeDtypeStruct((t, d), F32),
        grid=(t // tm,),
        in_specs=[row(d), _resident((1, d)), row(bw), row(bw), row(bw), row(bw),
                  _resident((d, N_BRANCH * d)), _resident((N_BRANCH, bw, d)), _resident((d, d))],
        out_specs=row(d),
        compiler_params=_params(),
        name="gated_merge",
    )(x2, norm_gain[None, :], *branches, w_gate.astype(BF16), w_branch.astype(BF16), w_out.astype(BF16))


def kernel(x, mem, positions, norm_ffn1, w_ffn1_in, w_ffn1_out, norm_mix, w_in, b_forget, moba_q_gain, moba_k_gain, fox_q_gain, fox_k_gain, norm_mem, w_mem_kv, mem_q_gain, mem_k_gain, pool_mix, pool_scale, w_branch, w_out, norm_ffn2, w_ffn2_in, w_ffn2_out):
    b, s, d = x.shape
    depth = norm_ffn1.shape[0]
    gate_off = w_in.shape[2] - N_BRANCH * d
    cos, sin = _rope_tables(positions)
    x2 = x.reshape(b * s, d)
    for l in range(depth):
        x2 = _ffn(x2, norm_ffn1[l], w_ffn1_in[l], w_ffn1_out[l])
        k_m, v_m = _mem_kv(mem, norm_mem[l], w_mem_kv[l], mem_k_gain[l])
        qa, ka, va, qc, kc, vc, c, ob, om, kmean = _project(
            x2.reshape(b, s, d), norm_mix[l], w_in[l, :, :gate_off], b_forget[l],
            moba_q_gain[l], moba_k_gain[l], fox_q_gain[l], fox_k_gain[l],
            cos, sin, pool_mix[l], pool_scale[l], k_m, v_m, mem_q_gain[l])
        oa = _moba_attention(qa, ka, va, kmean)
        oc = _fox_attention(qc, kc, vc, c)
        flat = lambda a: a.reshape(b * s, a.shape[2])
        x2 = _merge(x2, norm_mix[l], (flat(oa), flat(ob), flat(oc), flat(om)),
                    w_in[l, :, gate_off:], w_branch[l], w_out[l])
        x2 = _ffn(x2, norm_ffn2[l], w_ffn2_in[l], w_ffn2_out[l])
    return x2.reshape(b, s, d)
```
